```python
import jax, jax.numpy as jnp
from jax import lax
import numpy as np

D_MODEL = 2048
BATCH = 4
SEQ = 2048
DEPTH = 2

CHUNK = 64
Q_BLOCK = 128
MLA_HEADS = 8
QK_NOPE = 128
QK_ROPE = 64
V_HEAD = 128
Q_LORA = 512
KV_LORA = 256
MLA_WIDTH = MLA_HEADS * V_HEAD
CONV_WIDTH = D_MODEL - MLA_WIDTH
CONV_GROUPS = 8
CONV_K = 31
D_FF = 4 * D_MODEL
ROPE_THETA = 10000.0
LN_EPS = 1e-5
RMS_EPS = 1e-6
ALPHA = (2.0 * DEPTH) ** 0.25
BETA = (8.0 * DEPTH) ** -0.25
IN_COLS = Q_LORA + KV_LORA + QK_ROPE + 2 * CONV_WIDTH

kernel_name = "mla_conformer_conv_hybrid_deepnorm"


def layer_norm(x, g, b):
    xf = x.astype(jnp.float32)
    mu = jnp.mean(xf, axis=-1, keepdims=True)
    var = jnp.mean(jnp.square(xf - mu), axis=-1, keepdims=True)
    y = (xf - mu) * lax.rsqrt(var + LN_EPS)
    return (y * g.astype(jnp.float32) + b.astype(jnp.float32)).astype(x.dtype)


def rms_norm(x, g):
    xf = x.astype(jnp.float32)
    y = xf * lax.rsqrt(jnp.mean(jnp.square(xf), axis=-1, keepdims=True) + RMS_EPS)
    return (y * g.astype(jnp.float32)).astype(x.dtype)


def rope_tables(positions):
    inv_freq = ROPE_THETA ** (-jnp.arange(0, QK_ROPE, 2, dtype=jnp.float32) / QK_ROPE)
    ang = positions.astype(jnp.float32)[..., None] * inv_freq
    return jnp.cos(ang), jnp.sin(ang)


def apply_rope(x, cos, sin):
    xf = x.astype(jnp.float32)
    half = xf.shape[-1] // 2
    x1, x2 = xf[..., :half], xf[..., half:]
    return jnp.concatenate([x1 * cos - x2 * sin, x2 * cos + x1 * sin], axis=-1).astype(x.dtype)


def chunk_causal_mla(q_nope, q_rope, k_nope, k_rope, v):
    S = q_nope.shape[1]
    scale = (QK_NOPE + QK_ROPE) ** -0.5
    outs = []
    for blk in range(S // Q_BLOCK):
        q0 = blk * Q_BLOCK
        kend = q0 + Q_BLOCK
        s = (jnp.einsum('bqhd,bkhd->bhqk', q_nope[:, q0:kend], k_nope[:, :kend])
             + jnp.einsum('bqhr,bkr->bhqk', q_rope[:, q0:kend], k_rope[:, :kend]))
        s = s.astype(jnp.float32) * scale
        q_chunk = (q0 + jnp.arange(Q_BLOCK)) // CHUNK
        k_chunk = jnp.arange(kend) // CHUNK
        mask = k_chunk[None, :] <= q_chunk[:, None]
        s = jnp.where(mask, s, -1e30)
        p = jax.nn.softmax(s, axis=-1).astype(v.dtype)
        outs.append(jnp.einsum('bhqk,bkhd->bqhd', p, v[:, :kend]))
    return jnp.concatenate(outs, axis=1)


def causal_depthwise_conv(u, w, b):
    C = u.shape[-1]
    y = lax.conv_general_dilated(
        u, w[:, None, :].astype(u.dtype), window_strides=(1,),
        padding=((CONV_K - 1, 0),), dimension_numbers=('NWC', 'WIO', 'NWC'),
        feature_group_count=C)
    return y + b.astype(u.dtype)


def hybrid_layer(x, cos, sin, w_in, g_q, w_uq, g_kv, w_ukv, b_glu, w_dw, b_dw,
                 g_cln, b_cln, w_out, ln1_g, ln1_b, w1, w2, ln2_g, ln2_b):
    B, S, _ = x.shape
    z = x @ w_in
    o1 = Q_LORA
    o2 = o1 + KV_LORA
    o3 = o2 + QK_ROPE
    c_q, c_kv, k_rope, u = z[..., :o1], z[..., o1:o2], z[..., o2:o3], z[..., o3:]

    q = (rms_norm(c_q, g_q) @ w_uq).reshape(B, S, MLA_HEADS, QK_NOPE + QK_ROPE)
    q_nope = q[..., :QK_NOPE]
    q_rope = apply_rope(q[..., QK_NOPE:], cos[:, :, None, :], sin[:, :, None, :])
    kv = (rms_norm(c_kv, g_kv) @ w_ukv).reshape(B, S, MLA_HEADS, QK_NOPE + V_HEAD)
    k_nope, v = kv[..., :QK_NOPE], kv[..., QK_NOPE:]
    k_rope = apply_rope(k_rope, cos, sin)
    attn = chunk_causal_mla(q_nope, q_rope, k_nope, k_rope, v).reshape(B, S, MLA_WIDTH)

    u = u + b_glu
    h = u[..., :CONV_WIDTH] * jax.nn.sigmoid(u[..., CONV_WIDTH:])
    h = causal_depthwise_conv(h, w_dw, b_dw)
    h = jax.nn.silu(layer_norm(h, g_cln, b_cln))

    mix = jnp.concatenate([attn, h], axis=-1) @ w_out
    x = layer_norm(ALPHA * x + mix, ln1_g, ln1_b)

    f = jnp.square(jax.nn.relu(x @ w1)) @ w2
    return layer_norm(ALPHA * x + f, ln2_g, ln2_b)


def setup_inputs(seed: int = 0) -> dict:
    key = jax.random.key(seed)
    ks = jax.random.split(key, 24)
    f32 = jnp.float32
    nrm = lambda k, shape, s: jax.random.normal(k, shape, f32) * s
    L = DEPTH
    x = jax.random.normal(ks[0], (BATCH, SEQ, D_MODEL), f32)
    start = jax.random.randint(ks[1], (BATCH, 1), 0, 4096, dtype=jnp.int32)
    positions = (start + jnp.arange(SEQ, dtype=jnp.int32)[None, :]).astype(jnp.int32)
    v_scale = jnp.tile(jnp.concatenate([jnp.ones((QK_NOPE,), f32),
                                        jnp.full((V_HEAD,), BETA, f32)]), MLA_HEADS)
    return {
        "x": x,
        "positions": positions,
        "ln_in_g": 1.0 + nrm(ks[2], (D_MODEL,), 0.02),
        "ln_in_b": nrm(ks[3], (D_MODEL,), 0.02),
        "w_in": nrm(ks[4], (L, D_MODEL, IN_COLS), D_MODEL ** -0.5),
        "g_q": 1.0 + nrm(ks[5], (L, Q_LORA), 0.02),
        "w_uq": nrm(ks[6], (L, Q_LORA, MLA_HEADS * (QK_NOPE + QK_ROPE)), Q_LORA ** -0.5),
        "g_kv": 1.0 + nrm(ks[7], (L, KV_LORA), 0.02),
        "w_ukv": nrm(ks[8], (L, KV_LORA, MLA_HEADS * (QK_NOPE + V_HEAD)), KV_LORA ** -0.5) * v_scale,
        "b_glu": nrm(ks[9], (L, 2 * CONV_WIDTH), 0.02),
        "w_dw": nrm(ks[10], (L, CONV_K, CONV_WIDTH), CONV_K ** -0.5),
        "b_dw": nrm(ks[11], (L, CONV_WIDTH), 0.02),
        "g_cln": 1.0 + nrm(ks[12], (L, CONV_WIDTH), 0.02),
        "b_cln": nrm(ks[13], (L, CONV_WIDTH), 0.02),
        "w_out": nrm(ks[14], (L, D_MODEL, D_MODEL), BETA * D_MODEL ** -0.5),
        "ln1_g": 1.0 + nrm(ks[15], (L, D_MODEL), 0.02),
        "ln1_b": nrm(ks[16], (L, D_MODEL), 0.02),
        "w1": nrm(ks[17], (L, D_MODEL, D_FF), BETA * D_MODEL ** -0.5),
        "w2": nrm(ks[18], (L, D_FF, D_MODEL), BETA * D_FF ** -0.5),
        "ln2_g": 1.0 + nrm(ks[19], (L, D_MODEL), 0.02),
        "ln2_b": nrm(ks[20], (L, D_MODEL), 0.02),
    }


def reference(x, positions, ln_in_g, ln_in_b, w_in, g_q, w_uq, g_kv, w_ukv, b_glu,
              w_dw, b_dw, g_cln, b_cln, w_out, ln1_g, ln1_b, w1, w2, ln2_g, ln2_b):
    cos, sin = rope_tables(positions)
    h = layer_norm(x, ln_in_g, ln_in_b)
    for l in range(DEPTH):
        h = hybrid_layer(h, cos, sin, w_in[l], g_q[l], w_uq[l], g_kv[l], w_ukv[l],
                         b_glu[l], w_dw[l], b_dw[l], g_cln[l], b_cln[l], w_out[l],
                         ln1_g[l], ln1_b[l], w1[l], w2[l], ln2_g[l], ln2_b[l])
    return h
```

```python
import functools

import jax
import jax.numpy as jnp
from jax import lax
from jax.experimental import pallas as pl
from jax.experimental.pallas import tpu as pltpu

D_MODEL = 2048
CHUNK = 64
MLA_HEADS = 8
QK_NOPE = 128
QK_ROPE = 64
V_HEAD = 128
Q_LORA = 512
KV_LORA = 256
MLA_WIDTH = MLA_HEADS * V_HEAD
CONV_WIDTH = D_MODEL - MLA_WIDTH
CONV_K = 31
D_FF = 4 * D_MODEL
ROPE_THETA = 10000.0
LN_EPS = 1e-5
RMS_EPS = 1e-6

LANES = 128
SUBLANES = 8
HALF_ROPE = QK_ROPE // 2
A_COLS = Q_LORA + KV_LORA + 2 * QK_ROPE
CONV_HALO = 32

PREP_TM = 256
ATTN_TQ = 256
CONV_TS = 512
CONV_TR = 32
OUT_TM = 512
MLP_TM = 1024
MLP_TF = 512
MLP_TN = 512

BF16 = jnp.bfloat16
F32 = jnp.float32


def _layer_norm(x, g, b):
    mu = jnp.mean(x, axis=-1, keepdims=True)
    xc = x - mu
    var = jnp.mean(xc * xc, axis=-1, keepdims=True)
    return xc * lax.rsqrt(var + LN_EPS) * g + b


def _rms_norm(x, g):
    return x * lax.rsqrt(jnp.mean(x * x, axis=-1, keepdims=True) + RMS_EPS) * g


def _dot(a, b):
    return jnp.dot(a, b, preferred_element_type=F32)


def _const_spec(shape):
    return pl.BlockSpec(shape, lambda *_: (0,) * len(shape))


def _prep_kernel(apply_ln, *refs):
    if apply_ln:
        (x_ref, pos_ref, invf_ref, lng_ref, lnb_ref, wa_ref, wu_ref, bglu_ref, gq_ref, wq_ref,
         gkv_ref, wkv_ref, h_ref, qn_ref, qr_ref, kn_ref, v_ref, kr_ref, glu_ref) = refs
    else:
        (x_ref, pos_ref, invf_ref, wa_ref, wu_ref, bglu_ref, gq_ref, wq_ref,
         gkv_ref, wkv_ref, qn_ref, qr_ref, kn_ref, v_ref, kr_ref, glu_ref) = refs
    x = x_ref[...]
    if apply_ln:
        x = _layer_norm(x, lng_ref[...], lnb_ref[...])
        h_ref[...] = x
    xb = x.astype(BF16)

    ang = pos_ref[...].astype(F32) * invf_ref[...]
    cos_t = jnp.cos(ang)
    sin_t = jnp.sin(ang)

    za = _dot(xb, wa_ref[...])
    cq = za[:, :Q_LORA]
    ckv = za[:, Q_LORA:Q_LORA + KV_LORA]
    krb = za[:, Q_LORA + KV_LORA:]

    lane = lax.broadcasted_iota(jnp.int32, krb.shape, 1)
    t = krb * jnp.where(lane < QK_ROPE, cos_t, sin_t)
    kro = t + pltpu.roll(t, QK_ROPE, 1)
    kr_ref[...] = jnp.where(lane < QK_ROPE, kro, 0.0).astype(BF16)

    q = _dot(_rms_norm(cq, gq_ref[...]).astype(BF16), wq_ref[...])
    qn_ref[...] = q[:, :MLA_WIDTH].astype(BF16)
    for h in range(MLA_HEADS):
        lo = MLA_WIDTH + h * LANES
        sw = 2 * MLA_WIDTH + h * LANES
        qr_ref[:, h * LANES:(h + 1) * LANES] = (
            q[:, lo:lo + LANES] * cos_t + q[:, sw:sw + LANES] * sin_t).astype(BF16)

    kv = _dot(_rms_norm(ckv, gkv_ref[...]).astype(BF16), wkv_ref[...])
    kn_ref[...] = kv[:, :MLA_WIDTH].astype(BF16)
    v_ref[...] = kv[:, MLA_WIDTH:].astype(BF16)

    u = _dot(xb, wu_ref[...]) + bglu_ref[...]
    glu_ref[...] = u[:, :CONV_WIDTH] * jax.nn.sigmoid(u[:, CONV_WIDTH:])


def _prep(x, pos, invf, ln, wa, wu, bglu, gq, wq, gkv, wkv):
    m = x.shape[0]
    tm = PREP_TM
    apply_ln = ln is not None
    row = lambda w: pl.BlockSpec((tm, w), lambda i: (i, 0))
    in_specs = [row(D_MODEL), row(1), _const_spec((1, LANES))]
    args = [x, pos, invf]
    if apply_ln:
        in_specs += [_const_spec((1, D_MODEL))] * 2
        args += list(ln)
    in_specs += [_const_spec(wa.shape), _const_spec(wu.shape), _const_spec(bglu.shape),
                 _const_spec(gq.shape), _const_spec(wq.shape), _const_spec(gkv.shape),
                 _const_spec(wkv.shape)]
    args += [wa, wu, bglu, gq, wq, gkv, wkv]
    out_shape, out_specs = [], []
    if apply_ln:
        out_shape.append(jax.ShapeDtypeStruct((m, D_MODEL), F32))
        out_specs.append(row(D_MODEL))
    for _ in range(4):
        out_shape.append(jax.ShapeDtypeStruct((m, MLA_WIDTH), BF16))
        out_specs.append(row(MLA_WIDTH))
    out_shape.append(jax.ShapeDtypeStruct((m, LANES), BF16))
    out_specs.append(row(LANES))
    out_shape.append(jax.ShapeDtypeStruct((m, CONV_WIDTH), F32))
    out_specs.append(row(CONV_WIDTH))
    return pl.pallas_call(
        functools.partial(_prep_kernel, apply_ln),
        grid=(m // tm,),
        in_specs=in_specs,
        out_specs=out_specs,
        out_shape=out_shape,
        compiler_params=pltpu.CompilerParams(
            dimension_semantics=("arbitrary",), vmem_limit_bytes=56 * 1024 * 1024),
        name="prep_ln" if apply_ln else "prep",
    )(*args)


def _attn_kernel(qn_ref, qr_ref, kn_ref, kr_ref, v_ref, o_ref):
    i = pl.program_id(2)
    tq = ATTN_TQ
    scale = (QK_NOPE + QK_ROPE) ** -0.5
    q = jnp.concatenate([qn_ref[...], qr_ref[...]], axis=1)

    def block(j, carry, masked):
        m_prev, l_prev, acc = carry
        r0 = pl.multiple_of(j * tq, tq)
        k = jnp.concatenate([kn_ref[pl.ds(r0, tq), :], kr_ref[pl.ds(r0, tq), :]], axis=1)
        s = lax.dot_general(q, k, (((1,), (1,)), ((), ())), preferred_element_type=F32) * scale
        if masked:
            qc = lax.broadcasted_iota(jnp.int32, s.shape, 0) // CHUNK
            kc = lax.broadcasted_iota(jnp.int32, s.shape, 1) // CHUNK
            s = jnp.where(kc <= qc, s, -1e30)
        m_new = jnp.maximum(m_prev, jnp.max(s, axis=-1, keepdims=True))
        alpha = jnp.exp(m_prev - m_new)
        p = jnp.exp(s - m_new)
        l_new = alpha * l_prev + jnp.sum(p, axis=-1, keepdims=True)
        acc = alpha * acc + _dot(p.astype(BF16), v_ref[pl.ds(r0, tq), :])
        return m_new, l_new, acc

    init = (jnp.full((tq, 1), -jnp.inf, F32), jnp.zeros((tq, 1), F32), jnp.zeros((tq, V_HEAD), F32))
    carry = lax.fori_loop(0, i, lambda j, c: block(j, c, False), init)
    _, l_fin, acc = block(i, carry, True)
    o_ref[...] = (acc / l_fin).astype(o_ref.dtype)


def _attention(qn, qr, kn, kr, v, batch, seq):
    m = qn.shape[0]
    tq = ATTN_TQ
    nq = seq // tq
    qspec = pl.BlockSpec((tq, LANES), lambda b, h, i: (b * nq + i, h))
    kspec = pl.BlockSpec((seq, LANES), lambda b, h, i: (b, h))
    krspec = pl.BlockSpec((seq, LANES), lambda b, h, i: (b, 0))
    return pl.pallas_call(
        _attn_kernel,
        grid=(batch, MLA_HEADS, nq),
        in_specs=[qspec, qspec, kspec, krspec, kspec],
        out_specs=qspec,
        out_shape=jax.ShapeDtypeStruct((m, MLA_WIDTH), BF16),
        compiler_params=pltpu.CompilerParams(
            dimension_semantics=("arbitrary", "arbitrary", "arbitrary")),
        name="attention",
    )(qn, qr, kn, kr, v)


def _conv_kernel(x_ref, w_ref, b_ref, g_ref, beta_ref, o_ref, win_ref):
    ts = CONV_TS
    tr = CONV_TR

    @pl.when(pl.program_id(1) == 0)
    def _():
        win_ref[0:CONV_HALO, :] = jnp.zeros((CONV_HALO, CONV_WIDTH), F32)

    win_ref[CONV_HALO:CONV_HALO + ts, :] = x_ref[...]

    lead = CONV_HALO - (CONV_K - 1)

    def rows(r, _):
        r0 = pl.multiple_of(r * tr, tr)
        blocks = []
        for c in range(0, CONV_WIDTH, LANES):
            slab = win_ref[pl.ds(r0, tr + CONV_HALO), c:c + LANES]
            y = jnp.broadcast_to(b_ref[:, c:c + LANES], (tr, LANES))
            for b in range(SUBLANES):
                z = None
                nz = tr + (SUBLANES if b else 0)
                for a in range((lead + CONV_K - 1) // SUBLANES + 1):
                    k = SUBLANES * a + b - lead
                    if 0 <= k < CONV_K:
                        term = slab[SUBLANES * a:SUBLANES * a + nz, :] * w_ref[k:k + 1, c:c + LANES]
                        z = term if z is None else z + term
                y = y + z[b:b + tr, :]
            blocks.append(y)
        y = _layer_norm(jnp.concatenate(blocks, axis=1), g_ref[...], beta_ref[...])
        o_ref[pl.ds(r0, tr), :] = (y * jax.nn.sigmoid(y)).astype(o_ref.dtype)
        return 0

    lax.fori_loop(0, ts // tr, rows, 0)
    win_ref[0:CONV_HALO, :] = win_ref[ts:ts + CONV_HALO, :]


def _conv_module(glu, w, b, g, beta, batch, seq):
    m = glu.shape[0]
    ts = CONV_TS
    ns = seq // ts
    spec = pl.BlockSpec((ts, CONV_WIDTH), lambda bi, i: (bi * ns + i, 0))
    return pl.pallas_call(
        _conv_kernel,
        grid=(batch, ns),
        in_specs=[spec, _const_spec(w.shape), _const_spec(b.shape), _const_spec(g.shape),
                  _const_spec(beta.shape)],
        out_specs=spec,
        out_shape=jax.ShapeDtypeStruct((m, CONV_WIDTH), BF16),
        scratch_shapes=[pltpu.VMEM((ts + CONV_HALO, CONV_WIDTH), F32)],
        compiler_params=pltpu.CompilerParams(dimension_semantics=("arbitrary", "arbitrary")),
        name="conv_module",
    )(glu, w, b, g, beta)


def _outproj_kernel(alpha, a_ref, c_ref, x_ref, w_ref, g_ref, b_ref, o_ref):
    mix = _dot(a_ref[...], w_ref[0:MLA_WIDTH, :]) + _dot(c_ref[...], w_ref[MLA_WIDTH:D_MODEL, :])
    o_ref[...] = _layer_norm(alpha * x_ref[...] + mix, g_ref[...], b_ref[...])


def _outproj(attn, conv, x, w, g, b, alpha):
    m = x.shape[0]
    tm = OUT_TM
    row = lambda wd: pl.BlockSpec((tm, wd), lambda i: (i, 0))
    return pl.pallas_call(
        functools.partial(_outproj_kernel, alpha),
        grid=(m // tm,),
        in_specs=[row(MLA_WIDTH), row(CONV_WIDTH), row(D_MODEL), _const_spec(w.shape),
                  _const_spec(g.shape), _const_spec(b.shape)],
        out_specs=row(D_MODEL),
        out_shape=jax.ShapeDtypeStruct((m, D_MODEL), F32),
        compiler_params=pltpu.CompilerParams(
            dimension_semantics=("arbitrary",), vmem_limit_bytes=48 * 1024 * 1024),
        name="outproj",
    )(attn, conv, x, w, g, b)


def _mlp_kernel(alpha, x_ref, w1_ref, w2_ref, g_ref, b_ref, o_ref, xb_ref):
    f = pl.program_id(1)

    @pl.when(f == 0)
    def _():
        xb_ref[...] = x_ref[...].astype(BF16)

    h = _dot(xb_ref[...], w1_ref[...])
    h = jnp.maximum(h, 0.0)
    hb = (h * h).astype(BF16)

    for n in range(0, D_MODEL, MLP_TN):
        part = _dot(hb, w2_ref[:, n:n + MLP_TN])

        @pl.when(f == 0)
        def _():
            o_ref[:, n:n + MLP_TN] = part

        @pl.when(f != 0)
        def _():
            o_ref[:, n:n + MLP_TN] += part

    @pl.when(f == pl.num_programs(1) - 1)
    def _():
        o_ref[...] = _layer_norm(alpha * x_ref[...] + o_ref[...], g_ref[...], b_ref[...])


def _mlp(x, w1, w2, g, b, alpha):
    m = x.shape[0]
    tm, tf = MLP_TM, MLP_TF
    xspec = pl.BlockSpec((tm, D_MODEL), lambda i, f: (i, 0))
    return pl.pallas_call(
        functools.partial(_mlp_kernel, alpha),
        grid=(m // tm, D_FF // tf),
        in_specs=[xspec, pl.BlockSpec((D_MODEL, tf), lambda i, f: (0, f)),
                  pl.BlockSpec((tf, D_MODEL), lambda i, f: (f, 0)),
                  _const_spec(g.shape), _const_spec(b.shape)],
        out_specs=xspec,
        out_shape=jax.ShapeDtypeStruct((m, D_MODEL), F32),
        scratch_shapes=[pltpu.VMEM((tm, D_MODEL), BF16)],
        compiler_params=pltpu.CompilerParams(
            dimension_semantics=("arbitrary", "arbitrary"), vmem_limit_bytes=58 * 1024 * 1024),
        name="mlp",
    )(x, w1, w2, g, b)


def _swap_halves(w):
    return jnp.concatenate([-w[..., HALF_ROPE:], w[..., :HALF_ROPE]], axis=-1)


def _layer_weights(w_in, w_uq, w_ukv):
    o1, o2, o3 = Q_LORA, Q_LORA + KV_LORA, Q_LORA + KV_LORA + QK_ROPE
    w_kr = w_in[:, o2:o3]
    wa = jnp.concatenate([w_in[:, :o2], w_kr, _swap_halves(w_kr)], axis=1).astype(BF16)
    wu = w_in[:, o3:].astype(BF16)

    wq3 = w_uq.reshape(Q_LORA, MLA_HEADS, QK_NOPE + QK_ROPE)
    rope = wq3[:, :, QK_NOPE:]
    pad = jnp.zeros((Q_LORA, MLA_HEADS, LANES - QK_ROPE), w_uq.dtype)
    wq = jnp.concatenate([
        wq3[:, :, :QK_NOPE].reshape(Q_LORA, MLA_WIDTH),
        jnp.concatenate([rope, pad], axis=-1).reshape(Q_LORA, MLA_HEADS * LANES),
        jnp.concatenate([_swap_halves(rope), pad], axis=-1).reshape(Q_LORA, MLA_HEADS * LANES),
    ], axis=1).astype(BF16)

    wkv3 = w_ukv.reshape(KV_LORA, MLA_HEADS, QK_NOPE + V_HEAD)
    wkv = jnp.concatenate([wkv3[:, :, :QK_NOPE].reshape(KV_LORA, MLA_WIDTH),
                           wkv3[:, :, QK_NOPE:].reshape(KV_LORA, MLA_WIDTH)], axis=1).astype(BF16)
    return wa, wu, wq, wkv


def kernel(x, positions, ln_in_g, ln_in_b, w_in, g_q, w_uq, g_kv, w_ukv, b_glu, w_dw, b_dw, g_cln,
           b_cln, w_out, ln1_g, ln1_b, w1, w2, ln2_g, ln2_b):
    batch, seq, d = x.shape
    depth = w_in.shape[0]
    m = batch * seq
    alpha = (2.0 * depth) ** 0.25

    inv_freq = ROPE_THETA ** (-jnp.arange(0, QK_ROPE, 2, dtype=F32) / QK_ROPE)
    invf = jnp.tile(inv_freq, LANES // HALF_ROPE).reshape(1, LANES)
    pos = positions.reshape(m, 1)
    row = lambda a: a.reshape(1, -1)

    h = x.reshape(m, d)
    for l in range(depth):
        wa, wu, wq, wkv = _layer_weights(w_in[l], w_uq[l], w_ukv[l])
        ln = (row(ln_in_g), row(ln_in_b)) if l == 0 else None
        outs = _prep(h, pos, invf, ln, wa, wu, row(b_glu[l]), row(g_q[l]), wq, row(g_kv[l]), wkv)
        if l == 0:
            h, outs = outs[0], outs[1:]
        qn, qr, kn, v, kr, glu = outs
        attn = _attention(qn, qr, kn, kr, v, batch, seq)
        conv = _conv_module(glu, w_dw[l], row(b_dw[l]), row(g_cln[l]), row(b_cln[l]), batch, seq)
        h = _outproj(attn, conv, h, w_out[l].astype(BF16), row(ln1_g[l]), row(ln1_b[l]), alpha)
        h = _mlp(h, w1[l].astype(BF16), w2[l].astype(BF16), row(ln2_g[l]), row(ln2_b[l]), alpha)
    return h.reshape(batch, seq, d)
```

```python
import functools

import jax
import jax.numpy as jnp
from jax import lax
from jax.experimental import pallas as pl
from jax.experimental.pallas import tpu as pltpu

D_MODEL = 2048
CHUNK = 64
MLA_HEADS = 8
QK_NOPE = 128
QK_ROPE = 64
V_HEAD = 128
Q_LORA = 512
KV_LORA = 256
MLA_WIDTH = MLA_HEADS * V_HEAD
CONV_WIDTH = D_MODEL - MLA_WIDTH
CONV_K = 31
D_FF = 4 * D_MODEL
ROPE_THETA = 10000.0
LN_EPS = 1e-5
RMS_EPS = 1e-6

LANES = 128
SUBLANES = 8
HALF_ROPE = QK_ROPE // 2
A_COLS = Q_LORA + KV_LORA + 2 * QK_ROPE
CONV_HALO = 32

PREP_TM = 256
ATTN_TQ = 256
CONV_TS = 512
CONV_TR = 32
OUT_TM = 512
MLP_TM = 1024
MLP_TF = 512
MLP_TN = 512

BF16 = jnp.bfloat16
F32 = jnp.float32


def _layer_norm(x, g, b):
    mu = jnp.mean(x, axis=-1, keepdims=True)
    xc = x - mu
    var = jnp.mean(xc * xc, axis=-1, keepdims=True)
    return xc * lax.rsqrt(var + LN_EPS) * g + b


def _rms_norm(x, g):
    return x * lax.rsqrt(jnp.mean(x * x, axis=-1, keepdims=True) + RMS_EPS) * g


def _dot(a, b):
    return jnp.dot(a, b, preferred_element_type=F32)


def _const_spec(shape):
    return pl.BlockSpec(shape, lambda *_: (0,) * len(shape))


def _prep_kernel(apply_ln, *refs):
    if apply_ln:
        (x_ref, pos_ref, invf_ref, lng_ref, lnb_ref, wa_ref, wu_ref, bglu_ref, gq_ref, wq_ref,
         gkv_ref, wkv_ref, h_ref, qn_ref, qr_ref, kn_ref, v_ref, kr_ref, glu_ref) = refs
    else:
        (x_ref, pos_ref, invf_ref, wa_ref, wu_ref, bglu_ref, gq_ref, wq_ref,
         gkv_ref, wkv_ref, qn_ref, qr_ref, kn_ref, v_ref, kr_ref, glu_ref) = refs
    x = x_ref[...]
    if apply_ln:
        x = _layer_norm(x, lng_ref[...], lnb_ref[...])
        h_ref[...] = x
    xb = x.astype(BF16)

    ang = pos_ref[...].astype(F32) * invf_ref[...]
    cos_t = jnp.cos(ang)
    sin_t = jnp.sin(ang)

    za = _dot(xb, wa_ref[...])
    cq = za[:, :Q_LORA]
    ckv = za[:, Q_LORA:Q_LORA + KV_LORA]
    krb = za[:, Q_LORA + KV_LORA:]

    lane = lax.broadcasted_iota(jnp.int32, krb.shape, 1)
    t = krb * jnp.where(lane < QK_ROPE, cos_t, sin_t)
    kro = t + pltpu.roll(t, QK_ROPE, 1)
    kr_ref[...] = jnp.where(lane < QK_ROPE, kro, 0.0).astype(BF16)

    q = _dot(_rms_norm(cq, gq_ref[...]).astype(BF16), wq_ref[...])
    qn_ref[...] = q[:, :MLA_WIDTH].astype(BF16)
    for h in range(MLA_HEADS):
        lo = MLA_WIDTH + h * LANES
        sw = 2 * MLA_WIDTH + h * LANES
        qr_ref[:, h * LANES:(h + 1) * LANES] = (
            q[:, lo:lo + LANES] * cos_t + q[:, sw:sw + LANES] * sin_t).astype(BF16)

    kv = _dot(_rms_norm(ckv, gkv_ref[...]).astype(BF16), wkv_ref[...])
    kn_ref[...] = kv[:, :MLA_WIDTH].astype(BF16)
    v_ref[...] = kv[:, MLA_WIDTH:].astype(BF16)

    u = _dot(xb, wu_ref[...]) + bglu_ref[...]
    glu_ref[...] = u[:, :CONV_WIDTH] * jax.nn.sigmoid(u[:, CONV_WIDTH:])


def _prep(x, pos, invf, ln, wa, wu, bglu, gq, wq, gkv, wkv):
    m = x.shape[0]
    tm = PREP_TM
    apply_ln = ln is not None
    row = lambda w: pl.BlockSpec((tm, w), lambda i: (i, 0))
    in_specs = [row(D_MODEL), row(1), _const_spec((1, LANES))]
    args = [x, pos, invf]
    if apply_ln:
        in_specs += [_const_spec((1, D_MODEL))] * 2
        args += list(ln)
    in_specs += [_const_spec(wa.shape), _const_spec(wu.shape), _const_spec(bglu.shape),
                 _const_spec(gq.shape), _const_spec(wq.shape), _const_spec(gkv.shape),
                 _const_spec(wkv.shape)]
    args += [wa, wu, bglu, gq, wq, gkv, wkv]
    out_shape, out_specs = [], []
    if apply_ln:
        out_shape.append(jax.ShapeDtypeStruct((m, D_MODEL), F32))
        out_specs.append(row(D_MODEL))
    for _ in range(4):
        out_shape.append(jax.ShapeDtypeStruct((m, MLA_WIDTH), BF16))
        out_specs.append(row(MLA_WIDTH))
    out_shape.append(jax.ShapeDtypeStruct((m, LANES), BF16))
    out_specs.append(row(LANES))
    out_shape.append(jax.ShapeDtypeStruct((m, CONV_WIDTH), F32))
    out_specs.append(row(CONV_WIDTH))
    return pl.pallas_call(
        functools.partial(_prep_kernel, apply_ln),
        grid=(m // tm,),
        in_specs=in_specs,
        out_specs=out_specs,
        out_shape=out_shape,
        compiler_params=pltpu.CompilerParams(
            dimension_semantics=("arbitrary",), vmem_limit_bytes=56 * 1024 * 1024),
        name="prep_ln" if apply_ln else "prep",
    )(*args)


def _attn_kernel(qn_ref, qr_ref, kn_ref, kr_ref, v_ref, o_ref, s_ref, mx_ref, l_ref, acc_ref):
    i = pl.program_id(1)
    tq = ATTN_TQ
    c = (QK_NOPE + QK_ROPE) ** -0.5 * 1.4426950408889634
    hd = lambda h: slice(h * LANES, (h + 1) * LANES)

    mx_ref[...] = jnp.full(mx_ref.shape, -jnp.inf, F32)

    def scores(j, masked):
        r0 = pl.multiple_of(j * tq, tq)
        krj = kr_ref[pl.ds(r0, tq), :]
        for h in range(MLA_HEADS):
            q = jnp.concatenate([qn_ref[:, hd(h)], qr_ref[:, hd(h)]], axis=1)
            k = jnp.concatenate([kn_ref[pl.ds(r0, tq), hd(h)], krj], axis=1)
            s = lax.dot_general(q, k, (((1,), (1,)), ((), ())), preferred_element_type=F32) * c
            if masked:
                qc = lax.broadcasted_iota(jnp.int32, s.shape, 0) // CHUNK
                kc = lax.broadcasted_iota(jnp.int32, s.shape, 1) // CHUNK
                s = jnp.where(kc <= qc, s, -1e30)
            s_ref[h, j] = s
            mx_ref[h] = jnp.maximum(mx_ref[h], jnp.maximum(s[:, :LANES], s[:, LANES:]))

    def pass1(j, _):
        scores(j, False)
        return 0

    lax.fori_loop(0, i, pass1, 0)
    scores(i, True)

    for h in range(MLA_HEADS):
        mx_ref[h] = jnp.broadcast_to(jnp.max(mx_ref[h], axis=-1, keepdims=True), (tq, LANES))
    l_ref[...] = jnp.zeros(l_ref.shape, F32)
    acc_ref[...] = jnp.zeros(acc_ref.shape, F32)

    def pass2(j, _):
        r0 = pl.multiple_of(j * tq, tq)
        for h in range(MLA_HEADS):
            m = mx_ref[h]
            p = jnp.exp2(s_ref[h, j] - jnp.concatenate([m, m], axis=1))
            l_ref[h] += p[:, :LANES] + p[:, LANES:]
            acc_ref[h] += _dot(p.astype(BF16), v_ref[pl.ds(r0, tq), hd(h)])
        return 0

    lax.fori_loop(0, i + 1, pass2, 0)

    for h in range(MLA_HEADS):
        l = jnp.sum(l_ref[h], axis=-1, keepdims=True)
        o_ref[:, hd(h)] = (acc_ref[h] / l).astype(o_ref.dtype)


def _attention(qn, qr, kn, kr, v, batch, seq):
    m = qn.shape[0]
    tq = ATTN_TQ
    nq = seq // tq
    qspec = pl.BlockSpec((tq, MLA_WIDTH), lambda b, i: (b * nq + i, 0))
    kspec = pl.BlockSpec((seq, MLA_WIDTH), lambda b, i: (b, 0))
    krspec = pl.BlockSpec((seq, LANES), lambda b, i: (b, 0))
    head_acc = pltpu.VMEM((MLA_HEADS, tq, LANES), F32)
    return pl.pallas_call(
        _attn_kernel,
        grid=(batch, nq),
        in_specs=[qspec, qspec, kspec, krspec, kspec],
        out_specs=qspec,
        out_shape=jax.ShapeDtypeStruct((m, MLA_WIDTH), BF16),
        scratch_shapes=[pltpu.VMEM((MLA_HEADS, nq, tq, tq), F32), head_acc, head_acc, head_acc],
        compiler_params=pltpu.CompilerParams(
            dimension_semantics=("arbitrary", "arbitrary"), vmem_limit_bytes=56 * 1024 * 1024),
        name="attention",
    )(qn, qr, kn, kr, v)


def _conv_kernel(x_ref, w_ref, b_ref, g_ref, beta_ref, o_ref, win_ref):
    ts = CONV_TS
    tr = CONV_TR

    @pl.when(pl.program_id(1) == 0)
    def _():
        win_ref[0:CONV_HALO, :] = jnp.zeros((CONV_HALO, CONV_WIDTH), F32)

    win_ref[CONV_HALO:CONV_HALO + ts, :] = x_ref[...]

    lead = CONV_HALO - (CONV_K - 1)

    def rows(r, _):
        r0 = pl.multiple_of(r * tr, tr)
        blocks = []
        for c in range(0, CONV_WIDTH, LANES):
            slab = win_ref[pl.ds(r0, tr + CONV_HALO), c:c + LANES]
            y = jnp.broadcast_to(b_ref[:, c:c + LANES], (tr, LANES))
            for b in range(SUBLANES):
                z = None
                nz = tr + (SUBLANES if b else 0)
                for a in range((lead + CONV_K - 1) // SUBLANES + 1):
                    k = SUBLANES * a + b - lead
                    if 0 <= k < CONV_K:
                        term = slab[SUBLANES * a:SUBLANES * a + nz, :] * w_ref[k:k + 1, c:c + LANES]
                        z = term if z is None else z + term
                y = y + z[b:b + tr, :]
            blocks.append(y)
        y = _layer_norm(jnp.concatenate(blocks, axis=1), g_ref[...], beta_ref[...])
        o_ref[pl.ds(r0, tr), :] = (y * jax.nn.sigmoid(y)).astype(o_ref.dtype)
        return 0

    lax.fori_loop(0, ts // tr, rows, 0)
    win_ref[0:CONV_HALO, :] = win_ref[ts:ts + CONV_HALO, :]


def _conv_module(glu, w, b, g, beta, batch, seq):
    m = glu.shape[0]
    ts = CONV_TS
    ns = seq // ts
    spec = pl.BlockSpec((ts, CONV_WIDTH), lambda bi, i: (bi * ns + i, 0))
    return pl.pallas_call(
        _conv_kernel,
        grid=(batch, ns),
        in_specs=[spec, _const_spec(w.shape), _const_spec(b.shape), _const_spec(g.shape),
                  _const_spec(beta.shape)],
        out_specs=spec,
        out_shape=jax.ShapeDtypeStruct((m, CONV_WIDTH), BF16),
        scratch_shapes=[pltpu.VMEM((ts + CONV_HALO, CONV_WIDTH), F32)],
        compiler_params=pltpu.CompilerParams(dimension_semantics=("arbitrary", "arbitrary")),
        name="conv_module",
    )(glu, w, b, g, beta)


def _outproj_kernel(alpha, a_ref, c_ref, x_ref, w_ref, g_ref, b_ref, o_ref):
    mix = _dot(a_ref[...], w_ref[0:MLA_WIDTH, :]) + _dot(c_ref[...], w_ref[MLA_WIDTH:D_MODEL, :])
    o_ref[...] = _layer_norm(alpha * x_ref[...] + mix, g_ref[...], b_ref[...])


def _outproj(attn, conv, x, w, layer, g, b, alpha):
    m = x.shape[0]
    tm = OUT_TM
    row = lambda wd: pl.BlockSpec((tm, wd), lambda i: (i, 0))
    return pl.pallas_call(
        functools.partial(_outproj_kernel, alpha),
        grid=(m // tm,),
        in_specs=[row(MLA_WIDTH), row(CONV_WIDTH), row(D_MODEL),
                  pl.BlockSpec((None, D_MODEL, D_MODEL), lambda i: (layer, 0, 0)),
                  _const_spec(g.shape), _const_spec(b.shape)],
        out_specs=row(D_MODEL),
        out_shape=jax.ShapeDtypeStruct((m, D_MODEL), F32),
        compiler_params=pltpu.CompilerParams(
            dimension_semantics=("arbitrary",), vmem_limit_bytes=48 * 1024 * 1024),
        name="outproj",
    )(attn, conv, x, w, g, b)


def _mlp_kernel(alpha, x_ref, w1_ref, w2_ref, g_ref, b_ref, o_ref, xb_ref):
    f = pl.program_id(1)

    @pl.when(f == 0)
    def _():
        x = x_ref[...]
        xb_ref[...] = x.astype(BF16)
        o_ref[...] = alpha * x

    h = _dot(xb_ref[...], w1_ref[...])
    h = jnp.maximum(h, 0.0)
    hb = (h * h).astype(BF16)
    for n in range(0, D_MODEL, MLP_TN):
        o_ref[:, n:n + MLP_TN] += _dot(hb, w2_ref[:, n:n + MLP_TN])

    @pl.when(f == pl.num_programs(1) - 1)
    def _():
        o_ref[...] = _layer_norm(o_ref[...], g_ref[...], b_ref[...])


def _mlp(x, w1, w2, layer, g, b, alpha):
    m = x.shape[0]
    tm, tf = MLP_TM, MLP_TF
    xspec = pl.BlockSpec((tm, D_MODEL), lambda i, f: (i, 0))
    return pl.pallas_call(
        functools.partial(_mlp_kernel, alpha),
        grid=(m // tm, D_FF // tf),
        in_specs=[xspec, pl.BlockSpec((None, D_MODEL, tf), lambda i, f: (layer, 0, f)),
                  pl.BlockSpec((None, tf, D_MODEL), lambda i, f: (layer, f, 0)),
                  _const_spec(g.shape), _const_spec(b.shape)],
        out_specs=xspec,
        out_shape=jax.ShapeDtypeStruct((m, D_MODEL), F32),
        scratch_shapes=[pltpu.VMEM((tm, D_MODEL), BF16)],
        compiler_params=pltpu.CompilerParams(
            dimension_semantics=("arbitrary", "arbitrary"), vmem_limit_bytes=58 * 1024 * 1024),
        name="mlp",
    )(x, w1, w2, g, b)


def _swap_halves(w):
    return jnp.concatenate([-w[..., HALF_ROPE:], w[..., :HALF_ROPE]], axis=-1)


def _layer_weights(w_in, w_uq, w_ukv):
    o1, o2, o3 = Q_LORA, Q_LORA + KV_LORA, Q_LORA + KV_LORA + QK_ROPE
    w_kr = w_in[:, o2:o3]
    wa = jnp.concatenate([w_in[:, :o2], w_kr, _swap_halves(w_kr)], axis=1).astype(BF16)
    wu = w_in[:, o3:].astype(BF16)

    wq3 = w_uq.reshape(Q_LORA, MLA_HEADS, QK_NOPE + QK_ROPE)
    rope = wq3[:, :, QK_NOPE:]
    pad = jnp.zeros((Q_LORA, MLA_HEADS, LANES - QK_ROPE), w_uq.dtype)
    wq = jnp.concatenate([
        wq3[:, :, :QK_NOPE].reshape(Q_LORA, MLA_WIDTH),
        jnp.concatenate([rope, pad], axis=-1).reshape(Q_LORA, MLA_HEADS * LANES),
        jnp.concatenate([_swap_halves(rope), pad], axis=-1).reshape(Q_LORA, MLA_HEADS * LANES),
    ], axis=1).astype(BF16)

    wkv3 = w_ukv.reshape(KV_LORA, MLA_HEADS, QK_NOPE + V_HEAD)
    wkv = jnp.concatenate([wkv3[:, :, :QK_NOPE].reshape(KV_LORA, MLA_WIDTH),
                           wkv3[:, :, QK_NOPE:].reshape(KV_LORA, MLA_WIDTH)], axis=1).astype(BF16)
    return wa, wu, wq, wkv


def kernel(x, positions, ln_in_g, ln_in_b, w_in, g_q, w_uq, g_kv, w_ukv, b_glu, w_dw, b_dw, g_cln,
           b_cln, w_out, ln1_g, ln1_b, w1, w2, ln2_g, ln2_b):
    batch, seq, d = x.shape
    depth = w_in.shape[0]
    m = batch * seq
    alpha = (2.0 * depth) ** 0.25

    inv_freq = ROPE_THETA ** (-jnp.arange(0, QK_ROPE, 2, dtype=F32) / QK_ROPE)
    invf = jnp.tile(inv_freq, LANES // HALF_ROPE).reshape(1, LANES)
    pos = positions.reshape(m, 1)
    row = lambda a: a.reshape(1, -1)

    w_out_b, w1_b, w2_b = w_out.astype(BF16), w1.astype(BF16), w2.astype(BF16)
    h = x.reshape(m, d)
    for l in range(depth):
        wa, wu, wq, wkv = _layer_weights(w_in[l], w_uq[l], w_ukv[l])
        ln = (row(ln_in_g), row(ln_in_b)) if l == 0 else None
        outs = _prep(h, pos, invf, ln, wa, wu, row(b_glu[l]), row(g_q[l]), wq, row(g_kv[l]), wkv)
        if l == 0:
            h, outs = outs[0], outs[1:]
        qn, qr, kn, v, kr, glu = outs
        attn = _attention(qn, qr, kn, kr, v, batch, seq)
        conv = _conv_module(glu, w_dw[l], row(b_dw[l]), row(g_cln[l]), row(b_cln[l]), batch, seq)
        h = _outproj(attn, conv, h, w_out_b, l, row(ln1_g[l]), row(ln1_b[l]), alpha)
        h = _mlp(h, w1_b, w2_b, l, row(ln2_g[l]), row(ln2_b[l]), alpha)
    return h.reshape(batch, seq, d)
```

```python
import functools

import jax
import jax.numpy as jnp
from jax import lax
from jax.experimental import pallas as pl
from jax.experimental.pallas import tpu as pltpu

D_MODEL = 2048
CHUNK = 64
MLA_HEADS = 8
QK_NOPE = 128
QK_ROPE = 64
V_HEAD = 128
Q_LORA = 512
KV_LORA = 256
MLA_WIDTH = MLA_HEADS * V_HEAD
CONV_WIDTH = D_MODEL - MLA_WIDTH
CONV_K = 31
D_FF = 4 * D_MODEL
ROPE_THETA = 10000.0
LN_EPS = 1e-5
RMS_EPS = 1e-6

LANES = 128
SUBLANES = 8
HALF_ROPE = QK_ROPE // 2
A_COLS = Q_LORA + KV_LORA + 2 * QK_ROPE
CONV_HALO = 32

PREP_TM = 256
ATTN_TQ = 256
CONV_TS = 512
CONV_TR = 32
OUT_TM = 512
OUT_TR = 256
MLP_TM = 1024
MLP_TF = 512
MLP_TN = 512

BF16 = jnp.bfloat16
F32 = jnp.float32


def _layer_norm(x, g, b):
    mu = jnp.mean(x, axis=-1, keepdims=True)
    xc = x - mu
    var = jnp.mean(xc * xc, axis=-1, keepdims=True)
    return xc * lax.rsqrt(var + LN_EPS) * g + b


def _rms_norm(x, g):
    return x * lax.rsqrt(jnp.mean(x * x, axis=-1, keepdims=True) + RMS_EPS) * g


def _dot(a, b):
    return jnp.dot(a, b, preferred_element_type=F32)


def _const_spec(shape):
    return pl.BlockSpec(shape, lambda *_: (0,) * len(shape))


def _prep_kernel(apply_ln, *refs):
    if apply_ln:
        (x_ref, pos_ref, invf_ref, lng_ref, lnb_ref, wa_ref, wu_ref, bglu_ref, gq_ref, wq_ref,
         gkv_ref, wkv_ref, h_ref, q_ref, k_ref, v_ref, glu_ref) = refs
    else:
        (x_ref, pos_ref, invf_ref, wa_ref, wu_ref, bglu_ref, gq_ref, wq_ref,
         gkv_ref, wkv_ref, q_ref, k_ref, v_ref, glu_ref) = refs
    nope = lambda h: slice(2 * h * LANES, (2 * h + 1) * LANES)
    rope = lambda h: slice((2 * h + 1) * LANES, (2 * h + 2) * LANES)
    x = x_ref[...]
    if apply_ln:
        x = _layer_norm(x, lng_ref[...], lnb_ref[...])
        h_ref[...] = x
    xb = x.astype(BF16)

    ang = pos_ref[...].astype(F32) * invf_ref[...]
    cos_t = jnp.cos(ang)
    sin_t = jnp.sin(ang)

    za = _dot(xb, wa_ref[...])
    cq = za[:, :Q_LORA]
    ckv = za[:, Q_LORA:Q_LORA + KV_LORA]
    krb = za[:, Q_LORA + KV_LORA:]

    lane = lax.broadcasted_iota(jnp.int32, krb.shape, 1)
    t = krb * jnp.where(lane < QK_ROPE, cos_t, sin_t)
    kro = t + pltpu.roll(t, QK_ROPE, 1)
    kr = jnp.where(lane < QK_ROPE, kro, 0.0).astype(BF16)

    q = _dot(_rms_norm(cq, gq_ref[...]).astype(BF16), wq_ref[...])
    kv = _dot(_rms_norm(ckv, gkv_ref[...]).astype(BF16), wkv_ref[...])
    for h in range(MLA_HEADS):
        lo = MLA_WIDTH + h * LANES
        sw = 2 * MLA_WIDTH + h * LANES
        q_ref[:, nope(h)] = q[:, h * LANES:(h + 1) * LANES].astype(BF16)
        q_ref[:, rope(h)] = (q[:, lo:lo + LANES] * cos_t + q[:, sw:sw + LANES] * sin_t).astype(BF16)
        k_ref[:, nope(h)] = kv[:, h * LANES:(h + 1) * LANES].astype(BF16)
        k_ref[:, rope(h)] = kr
    v_ref[...] = kv[:, MLA_WIDTH:].astype(BF16)

    u = _dot(xb, wu_ref[...]) + bglu_ref[...]
    glu_ref[...] = u[:, :CONV_WIDTH] * jax.nn.sigmoid(u[:, CONV_WIDTH:])


def _prep(x, pos, invf, ln, wa, wu, bglu, gq, wq, gkv, wkv):
    m = x.shape[0]
    tm = PREP_TM
    apply_ln = ln is not None
    row = lambda w: pl.BlockSpec((tm, w), lambda i: (i, 0))
    in_specs = [row(D_MODEL), row(1), _const_spec((1, LANES))]
    args = [x, pos, invf]
    if apply_ln:
        in_specs += [_const_spec((1, D_MODEL))] * 2
        args += list(ln)
    in_specs += [_const_spec(wa.shape), _const_spec(wu.shape), _const_spec(bglu.shape),
                 _const_spec(gq.shape), _const_spec(wq.shape), _const_spec(gkv.shape),
                 _const_spec(wkv.shape)]
    args += [wa, wu, bglu, gq, wq, gkv, wkv]
    out_shape, out_specs = [], []
    if apply_ln:
        out_shape.append(jax.ShapeDtypeStruct((m, D_MODEL), F32))
        out_specs.append(row(D_MODEL))
    for width in (2 * MLA_WIDTH, 2 * MLA_WIDTH, MLA_WIDTH):
        out_shape.append(jax.ShapeDtypeStruct((m, width), BF16))
        out_specs.append(row(width))
    out_shape.append(jax.ShapeDtypeStruct((m, CONV_WIDTH), F32))
    out_specs.append(row(CONV_WIDTH))
    return pl.pallas_call(
        functools.partial(_prep_kernel, apply_ln),
        grid=(m // tm,),
        in_specs=in_specs,
        out_specs=out_specs,
        out_shape=out_shape,
        compiler_params=pltpu.CompilerParams(
            dimension_semantics=("arbitrary",), vmem_limit_bytes=56 * 1024 * 1024),
        name="prep_ln" if apply_ln else "prep",
    )(*args)


def _attn_kernel(q_ref, k_ref, v_ref, w1_ref, o_ref, w1b_ref, s_ref, mx_ref, l_ref, acc_ref):
    w1b_ref[...] = w1_ref[...].astype(BF16)

    i = pl.program_id(1)
    tq = ATTN_TQ
    c = (QK_NOPE + QK_ROPE) ** -0.5 * 1.4426950408889634
    hq = lambda h: slice(2 * h * LANES, 2 * (h + 1) * LANES)
    hv = lambda h: slice(h * LANES, (h + 1) * LANES)

    mx_ref[...] = jnp.full(mx_ref.shape, -jnp.inf, F32)

    def scores(j, masked):
        r0 = pl.multiple_of(j * tq, tq)
        for h in range(MLA_HEADS):
            s = lax.dot_general(q_ref[:, hq(h)], k_ref[pl.ds(r0, tq), hq(h)],
                                (((1,), (1,)), ((), ())), preferred_element_type=F32) * c
            if masked:
                qc = lax.broadcasted_iota(jnp.int32, s.shape, 0) // CHUNK
                kc = lax.broadcasted_iota(jnp.int32, s.shape, 1) // CHUNK
                s = jnp.where(kc <= qc, s, -1e30)
            s_ref[h, j] = s
            mx_ref[h] = jnp.maximum(mx_ref[h], jnp.maximum(s[:, :LANES], s[:, LANES:]))

    def weighted(j):
        r0 = pl.multiple_of(j * tq, tq)
        for h in range(MLA_HEADS):
            m = mx_ref[h]
            p = jnp.exp2(s_ref[h, j] - jnp.concatenate([m, m], axis=1))
            l_ref[h] += p[:, :LANES] + p[:, LANES:]
            acc_ref[h] += _dot(p.astype(BF16), v_ref[pl.ds(r0, tq), hv(h)])

    def pairs(fn, n):
        def two(t, _):
            fn(2 * t)
            fn(2 * t + 1)
            return 0
        lax.fori_loop(0, n // 2, two, 0)

        @pl.when(n % 2 == 1)
        def _():
            fn(n - 1)

    pairs(lambda j: scores(j, False), i)
    scores(i, True)

    for h in range(MLA_HEADS):
        mx_ref[h] = jnp.broadcast_to(jnp.max(mx_ref[h], axis=-1, keepdims=True), (tq, LANES))
    l_ref[...] = jnp.zeros(l_ref.shape, F32)
    acc_ref[...] = jnp.zeros(acc_ref.shape, F32)

    pairs(weighted, i + 1)

    for h in range(MLA_HEADS):
        l = jnp.sum(l_ref[h], axis=-1, keepdims=True)
        o_ref[:, hv(h)] = (acc_ref[h] / l).astype(o_ref.dtype)


def _attention(q, k, v, w1, layer, batch, seq):
    m = q.shape[0]
    tq = ATTN_TQ
    nq = seq // tq
    f1 = D_FF // (batch * nq)
    step = lambda b, i: b * nq + i
    head_acc = pltpu.VMEM((MLA_HEADS, tq, LANES), F32)
    return pl.pallas_call(
        _attn_kernel,
        grid=(batch, nq),
        in_specs=[pl.BlockSpec((tq, 2 * MLA_WIDTH), lambda b, i: (step(b, i), 0)),
                  pl.BlockSpec((seq, 2 * MLA_WIDTH), lambda b, i: (b, 0)),
                  pl.BlockSpec((seq, MLA_WIDTH), lambda b, i: (b, 0)),
                  pl.BlockSpec((None, D_MODEL, f1), lambda b, i: (layer, 0, step(b, i)))],
        out_specs=[pl.BlockSpec((tq, MLA_WIDTH), lambda b, i: (step(b, i), 0)),
                   pl.BlockSpec((D_MODEL, f1), lambda b, i: (0, step(b, i)))],
        out_shape=[jax.ShapeDtypeStruct((m, MLA_WIDTH), BF16),
                   jax.ShapeDtypeStruct((D_MODEL, D_FF), BF16)],
        scratch_shapes=[pltpu.VMEM((MLA_HEADS, nq, tq, tq), F32), head_acc, head_acc, head_acc],
        compiler_params=pltpu.CompilerParams(
            dimension_semantics=("arbitrary", "arbitrary"), vmem_limit_bytes=58 * 1024 * 1024),
        name="attention",
    )(q, k, v, w1)


def _conv_kernel(x_ref, w_ref, b_ref, g_ref, beta_ref, w2_ref, wo_ref, o_ref, w2b_ref, wob_ref,
                 win_ref):
    ts = CONV_TS
    tr = CONV_TR

    w2b_ref[...] = w2_ref[...].astype(BF16)
    wob_ref[...] = wo_ref[...].astype(BF16)

    @pl.when(pl.program_id(1) == 0)
    def _():
        win_ref[0:CONV_HALO, :] = jnp.zeros((CONV_HALO, CONV_WIDTH), F32)

    win_ref[CONV_HALO:CONV_HALO + ts, :] = x_ref[...]

    lead = CONV_HALO - (CONV_K - 1)

    def rows(r, _):
        r0 = pl.multiple_of(r * tr, tr)
        blocks = []
        for c in range(0, CONV_WIDTH, LANES):
            slab = win_ref[pl.ds(r0, tr + CONV_HALO), c:c + LANES]
            y = jnp.broadcast_to(b_ref[:, c:c + LANES], (tr, LANES))
            for b in range(SUBLANES):
                z = None
                nz = tr + (SUBLANES if b else 0)
                for a in range((lead + CONV_K - 1) // SUBLANES + 1):
                    k = SUBLANES * a + b - lead
                    if 0 <= k < CONV_K:
                        term = slab[SUBLANES * a:SUBLANES * a + nz, :] * w_ref[k:k + 1, c:c + LANES]
                        z = term if z is None else z + term
                y = y + z[b:b + tr, :]
            blocks.append(y)
        y = _layer_norm(jnp.concatenate(blocks, axis=1), g_ref[...], beta_ref[...])
        o_ref[pl.ds(r0, tr), :] = (y * jax.nn.sigmoid(y)).astype(o_ref.dtype)
        return 0

    lax.fori_loop(0, ts // tr, rows, 0)
    win_ref[0:CONV_HALO, :] = win_ref[ts:ts + CONV_HALO, :]


def _conv_module(glu, w, b, g, beta, w2, w_out, layer, batch, seq):
    m = glu.shape[0]
    ts = CONV_TS
    ns = seq // ts
    steps = batch * ns
    f2, fo = D_FF // steps, D_MODEL // steps
    step = lambda bi, i: bi * ns + i
    spec = pl.BlockSpec((ts, CONV_WIDTH), lambda bi, i: (step(bi, i), 0))
    return pl.pallas_call(
        _conv_kernel,
        grid=(batch, ns),
        in_specs=[spec, _const_spec(w.shape), _const_spec(b.shape), _const_spec(g.shape),
                  _const_spec(beta.shape),
                  pl.BlockSpec((None, f2, D_MODEL), lambda bi, i: (layer, step(bi, i), 0)),
                  pl.BlockSpec((None, fo, D_MODEL), lambda bi, i: (layer, step(bi, i), 0))],
        out_specs=[spec,
                   pl.BlockSpec((f2, D_MODEL), lambda bi, i: (step(bi, i), 0)),
                   pl.BlockSpec((fo, D_MODEL), lambda bi, i: (step(bi, i), 0))],
        out_shape=[jax.ShapeDtypeStruct((m, CONV_WIDTH), BF16),
                   jax.ShapeDtypeStruct((D_FF, D_MODEL), BF16),
                   jax.ShapeDtypeStruct((D_MODEL, D_MODEL), BF16)],
        scratch_shapes=[pltpu.VMEM((ts + CONV_HALO, CONV_WIDTH), F32)],
        compiler_params=pltpu.CompilerParams(
            dimension_semantics=("arbitrary", "arbitrary"), vmem_limit_bytes=40 * 1024 * 1024),
        name="conv_module",
    )(glu, w, b, g, beta, w2, w_out)


def _outproj_kernel(alpha, a_ref, c_ref, x_ref, w_ref, g_ref, b_ref, o_ref):
    for r in range(0, OUT_TM, OUT_TR):
        rows = slice(r, r + OUT_TR)
        mix = (_dot(a_ref[rows, :], w_ref[0:MLA_WIDTH, :])
               + _dot(c_ref[rows, :], w_ref[MLA_WIDTH:D_MODEL, :]))
        o_ref[rows, :] = _layer_norm(alpha * x_ref[rows, :] + mix, g_ref[...], b_ref[...])


def _outproj(attn, conv, x, w, g, b, alpha):
    m = x.shape[0]
    tm = OUT_TM
    row = lambda wd: pl.BlockSpec((tm, wd), lambda i: (i, 0))
    return pl.pallas_call(
        functools.partial(_outproj_kernel, alpha),
        grid=(m // tm,),
        in_specs=[row(MLA_WIDTH), row(CONV_WIDTH), row(D_MODEL), _const_spec(w.shape),
                  _const_spec(g.shape), _const_spec(b.shape)],
        out_specs=row(D_MODEL),
        out_shape=jax.ShapeDtypeStruct((m, D_MODEL), F32),
        compiler_params=pltpu.CompilerParams(
            dimension_semantics=("arbitrary",), vmem_limit_bytes=48 * 1024 * 1024),
        name="outproj",
    )(attn, conv, x, w, g, b)


def _mlp_kernel(alpha, x_ref, w1_ref, w2_ref, g_ref, b_ref, o_ref, xb_ref):
    f = pl.program_id(1)

    @pl.when(f == 0)
    def _():
        x = x_ref[...]
        xb_ref[...] = x.astype(BF16)
        o_ref[...] = alpha * x

    h = _dot(xb_ref[...], w1_ref[...])
    h = jnp.maximum(h, 0.0)
    hb = (h * h).astype(BF16)
    for n in range(0, D_MODEL, MLP_TN):
        o_ref[:, n:n + MLP_TN] += _dot(hb, w2_ref[:, n:n + MLP_TN])

    @pl.when(f == pl.num_programs(1) - 1)
    def _():
        o_ref[...] = _layer_norm(o_ref[...], g_ref[...], b_ref[...])


def _mlp(x, w1, w2, g, b, alpha):
    m = x.shape[0]
    tm, tf = MLP_TM, MLP_TF
    xspec = pl.BlockSpec((tm, D_MODEL), lambda i, f: (i, 0))
    return pl.pallas_call(
        functools.partial(_mlp_kernel, alpha),
        grid=(m // tm, D_FF // tf),
        in_specs=[xspec, pl.BlockSpec((D_MODEL, tf), lambda i, f: (0, f)),
                  pl.BlockSpec((tf, D_MODEL), lambda i, f: (f, 0)),
                  _const_spec(g.shape), _const_spec(b.shape)],
        out_specs=xspec,
        out_shape=jax.ShapeDtypeStruct((m, D_MODEL), F32),
        scratch_shapes=[pltpu.VMEM((tm, D_MODEL), BF16)],
        compiler_params=pltpu.CompilerParams(
            dimension_semantics=("arbitrary", "arbitrary"), vmem_limit_bytes=58 * 1024 * 1024),
        name="mlp",
    )(x, w1, w2, g, b)


def _swap_halves(w):
    return jnp.concatenate([-w[..., HALF_ROPE:], w[..., :HALF_ROPE]], axis=-1)


def _layer_weights(w_in, w_uq, w_ukv):
    o1, o2, o3 = Q_LORA, Q_LORA + KV_LORA, Q_LORA + KV_LORA + QK_ROPE
    w_kr = w_in[:, o2:o3]
    wa = jnp.concatenate([w_in[:, :o2], w_kr, _swap_halves(w_kr)], axis=1).astype(BF16)
    wu = w_in[:, o3:].astype(BF16)

    wq3 = w_uq.reshape(Q_LORA, MLA_HEADS, QK_NOPE + QK_ROPE)
    rope = wq3[:, :, QK_NOPE:]
    pad = jnp.zeros((Q_LORA, MLA_HEADS, LANES - QK_ROPE), w_uq.dtype)
    wq = jnp.concatenate([
        wq3[:, :, :QK_NOPE].reshape(Q_LORA, MLA_WIDTH),
        jnp.concatenate([rope, pad], axis=-1).reshape(Q_LORA, MLA_HEADS * LANES),
        jnp.concatenate([_swap_halves(rope), pad], axis=-1).reshape(Q_LORA, MLA_HEADS * LANES),
    ], axis=1).astype(BF16)

    wkv3 = w_ukv.reshape(KV_LORA, MLA_HEADS, QK_NOPE + V_HEAD)
    wkv = jnp.concatenate([wkv3[:, :, :QK_NOPE].reshape(KV_LORA, MLA_WIDTH),
                           wkv3[:, :, QK_NOPE:].reshape(KV_LORA, MLA_WIDTH)], axis=1).astype(BF16)
    return wa, wu, wq, wkv


def kernel(x, positions, ln_in_g, ln_in_b, w_in, g_q, w_uq, g_kv, w_ukv, b_glu, w_dw, b_dw, g_cln,
           b_cln, w_out, ln1_g, ln1_b, w1, w2, ln2_g, ln2_b):
    batch, seq, d = x.shape
    depth = w_in.shape[0]
    m = batch * seq
    alpha = (2.0 * depth) ** 0.25

    inv_freq = ROPE_THETA ** (-jnp.arange(0, QK_ROPE, 2, dtype=F32) / QK_ROPE)
    invf = jnp.tile(inv_freq, LANES // HALF_ROPE).reshape(1, LANES)
    pos = positions.reshape(m, 1)
    row = lambda a: a.reshape(1, -1)

    h = x.reshape(m, d)
    for l in range(depth):
        wa, wu, wq, wkv = _layer_weights(w_in[l], w_uq[l], w_ukv[l])
        ln = (row(ln_in_g), row(ln_in_b)) if l == 0 else None
        outs = _prep(h, pos, invf, ln, wa, wu, row(b_glu[l]), row(g_q[l]), wq, row(g_kv[l]), wkv)
        if l == 0:
            h, outs = outs[0], outs[1:]
        q, k, v, glu = outs
        attn, w1_b = _attention(q, k, v, w1, l, batch, seq)
        conv, w2_b, wo_b = _conv_module(glu, w_dw[l], row(b_dw[l]), row(g_cln[l]), row(b_cln[l]),
                                        w2, w_out, l, batch, seq)
        h = _outproj(attn, conv, h, wo_b, row(ln1_g[l]), row(ln1_b[l]), alpha)
        h = _mlp(h, w1_b, w2_b, row(ln2_g[l]), row(ln2_b[l]), alpha)
    return h.reshape(batch, seq, d)
```

```python
import functools

import jax
import jax.numpy as jnp
from jax import lax
from jax.experimental import pallas as pl
from jax.experimental.pallas import tpu as pltpu

D_MODEL = 2048
CHUNK = 64
MLA_HEADS = 8
QK_NOPE = 128
QK_ROPE = 64
V_HEAD = 128
Q_LORA = 512
KV_LORA = 256
MLA_WIDTH = MLA_HEADS * V_HEAD
CONV_WIDTH = D_MODEL - MLA_WIDTH
CONV_K = 31
D_FF = 4 * D_MODEL
ROPE_THETA = 10000.0
LN_EPS = 1e-5
RMS_EPS = 1e-6

LANES = 128
SUBLANES = 8
HALF_ROPE = QK_ROPE // 2
A_COLS = Q_LORA + KV_LORA + 2 * QK_ROPE
CONV_HALO = 32

PREP_TM = 256
PREP_TC = 128
ATTN_TQ = 256
CONV_TR = 32
OUT_TM = 512
OUT_TR = 256
MLP_TM = 1024
MLP_TF = 512
MLP_TN = 512

BF16 = jnp.bfloat16
F32 = jnp.float32


def _layer_norm(x, g, b):
    mu = jnp.mean(x, axis=-1, keepdims=True)
    xc = x - mu
    var = jnp.mean(xc * xc, axis=-1, keepdims=True)
    return xc * lax.rsqrt(var + LN_EPS) * g + b


def _rms_norm(x, g):
    return x * lax.rsqrt(jnp.mean(x * x, axis=-1, keepdims=True) + RMS_EPS) * g


def _dot(a, b):
    return jnp.dot(a, b, preferred_element_type=F32)


def _const_spec(shape):
    return pl.BlockSpec(shape, lambda *_: (0,) * len(shape))


def _conv_rows(win_ref, r0, tr, w_ref, b_ref):
    lead = CONV_HALO - (CONV_K - 1)
    blocks = []
    for c in range(0, CONV_WIDTH, LANES):
        slab = win_ref[r0:r0 + tr + CONV_HALO, c:c + LANES]
        y = jnp.broadcast_to(b_ref[:, c:c + LANES], (tr, LANES))
        for b in range(SUBLANES):
            z = None
            nz = tr + (SUBLANES if b else 0)
            for a in range((lead + CONV_K - 1) // SUBLANES + 1):
                k = SUBLANES * a + b - lead
                if 0 <= k < CONV_K:
                    term = slab[SUBLANES * a:SUBLANES * a + nz, :] * w_ref[k:k + 1, c:c + LANES]
                    z = term if z is None else z + term
            y = y + z[b:b + tr, :]
        blocks.append(y)
    return jnp.concatenate(blocks, axis=1)


def _prep_kernel(apply_ln, tiles_per_seq, *refs):
    if apply_ln:
        (x_ref, pos_ref, invf_ref, lng_ref, lnb_ref, wa_ref, wu_ref, bglu_ref, gq_ref, wq_ref,
         gkv_ref, wkv_ref, wdw_ref, bdw_ref, gcln_ref, bcln_ref, w2_ref, wo_ref,
         h_ref, q_ref, k_ref, v_ref, conv_ref, w2b_ref, wob_ref, win_ref) = refs
    else:
        (x_ref, pos_ref, invf_ref, wa_ref, wu_ref, bglu_ref, gq_ref, wq_ref,
         gkv_ref, wkv_ref, wdw_ref, bdw_ref, gcln_ref, bcln_ref, w2_ref, wo_ref,
         q_ref, k_ref, v_ref, conv_ref, w2b_ref, wob_ref, win_ref) = refs
    tm = PREP_TM

    @pl.when(pl.program_id(0) % tiles_per_seq == 0)
    def _():
        win_ref[0:CONV_HALO, :] = jnp.zeros((CONV_HALO, CONV_WIDTH), F32)

    w2b_ref[...] = w2_ref[...].astype(BF16)
    wob_ref[...] = wo_ref[...].astype(BF16)

    nope = lambda h: slice(2 * h * LANES, (2 * h + 1) * LANES)
    rope = lambda h: slice((2 * h + 1) * LANES, (2 * h + 2) * LANES)
    x = x_ref[...]
    if apply_ln:
        x = _layer_norm(x, lng_ref[...], lnb_ref[...])
        h_ref[...] = x
    xb = x.astype(BF16)

    for c0 in range(0, tm, PREP_TC):
        u = _dot(xb[c0:c0 + PREP_TC, :], wu_ref[...]) + bglu_ref[...]
        win_ref[CONV_HALO + c0:CONV_HALO + c0 + PREP_TC, :] = (
            u[:, :CONV_WIDTH] * jax.nn.sigmoid(u[:, CONV_WIDTH:]))
        for r0 in range(c0, c0 + PREP_TC, CONV_TR):
            y = _layer_norm(_conv_rows(win_ref, r0, CONV_TR, wdw_ref, bdw_ref),
                            gcln_ref[...], bcln_ref[...])
            conv_ref[r0:r0 + CONV_TR, :] = (y * jax.nn.sigmoid(y)).astype(BF16)

    ang = pos_ref[...].astype(F32) * invf_ref[...]
    cos_t = jnp.cos(ang)
    sin_t = jnp.sin(ang)

    za = _dot(xb, wa_ref[...])
    cq = za[:, :Q_LORA]
    ckv = za[:, Q_LORA:Q_LORA + KV_LORA]
    krb = za[:, Q_LORA + KV_LORA:]

    lane = lax.broadcasted_iota(jnp.int32, krb.shape, 1)
    t = krb * jnp.where(lane < QK_ROPE, cos_t, sin_t)
    kro = t + pltpu.roll(t, QK_ROPE, 1)
    kr = jnp.where(lane < QK_ROPE, kro, 0.0).astype(BF16)

    q = _dot(_rms_norm(cq, gq_ref[...]).astype(BF16), wq_ref[...])
    kv = _dot(_rms_norm(ckv, gkv_ref[...]).astype(BF16), wkv_ref[...])
    for h in range(MLA_HEADS):
        lo = MLA_WIDTH + h * LANES
        sw = 2 * MLA_WIDTH + h * LANES
        q_ref[:, nope(h)] = q[:, h * LANES:(h + 1) * LANES].astype(BF16)
        q_ref[:, rope(h)] = (q[:, lo:lo + LANES] * cos_t + q[:, sw:sw + LANES] * sin_t).astype(BF16)
        k_ref[:, nope(h)] = kv[:, h * LANES:(h + 1) * LANES].astype(BF16)
        k_ref[:, rope(h)] = kr
    v_ref[...] = kv[:, MLA_WIDTH:].astype(BF16)

    win_ref[0:CONV_HALO, :] = win_ref[tm:tm + CONV_HALO, :]


def _prep(x, pos, invf, ln, wa, wu, bglu, gq, wq, gkv, wkv, wdw, bdw, gcln, bcln, w2, w_out, layer,
          seq):
    m = x.shape[0]
    tm = PREP_TM
    steps = m // tm
    f2, fo = D_FF // steps, D_MODEL // steps
    apply_ln = ln is not None
    row = lambda w: pl.BlockSpec((tm, w), lambda i: (i, 0))
    in_specs = [row(D_MODEL), row(1), _const_spec((1, LANES))]
    args = [x, pos, invf]
    if apply_ln:
        in_specs += [_const_spec((1, D_MODEL))] * 2
        args += list(ln)
    consts = [wa, wu, bglu, gq, wq, gkv, wkv, wdw, bdw, gcln, bcln]
    in_specs += [_const_spec(a.shape) for a in consts]
    in_specs += [pl.BlockSpec((None, f2, D_MODEL), lambda i: (layer, i, 0)),
                 pl.BlockSpec((None, fo, D_MODEL), lambda i: (layer, i, 0))]
    args += consts + [w2, w_out]
    out_shape, out_specs = [], []
    if apply_ln:
        out_shape.append(jax.ShapeDtypeStruct((m, D_MODEL), F32))
        out_specs.append(row(D_MODEL))
    for width in (2 * MLA_WIDTH, 2 * MLA_WIDTH, MLA_WIDTH, CONV_WIDTH):
        out_shape.append(jax.ShapeDtypeStruct((m, width), BF16))
        out_specs.append(row(width))
    out_shape += [jax.ShapeDtypeStruct((D_FF, D_MODEL), BF16),
                  jax.ShapeDtypeStruct((D_MODEL, D_MODEL), BF16)]
    out_specs += [pl.BlockSpec((f2, D_MODEL), lambda i: (i, 0)),
                  pl.BlockSpec((fo, D_MODEL), lambda i: (i, 0))]
    return pl.pallas_call(
        functools.partial(_prep_kernel, apply_ln, seq // tm),
        grid=(steps,),
        in_specs=in_specs,
        out_specs=out_specs,
        out_shape=out_shape,
        scratch_shapes=[pltpu.VMEM((tm + CONV_HALO, CONV_WIDTH), F32)],
        compiler_params=pltpu.CompilerParams(
            dimension_semantics=("arbitrary",), vmem_limit_bytes=58 * 1024 * 1024),
        name="prep_ln" if apply_ln else "prep",
    )(*args)


def _attn_kernel(q_ref, k_ref, v_ref, w1_ref, o_ref, w1b_ref, s_ref, mx_ref, l_ref, acc_ref):
    w1b_ref[...] = w1_ref[...].astype(BF16)

    i = pl.program_id(1)
    tq = ATTN_TQ
    c = (QK_NOPE + QK_ROPE) ** -0.5 * 1.4426950408889634
    hq = lambda h: slice(2 * h * LANES, 2 * (h + 1) * LANES)
    hv = lambda h: slice(h * LANES, (h + 1) * LANES)

    mx_ref[...] = jnp.full(mx_ref.shape, -jnp.inf, F32)

    def scores(j, masked):
        r0 = pl.multiple_of(j * tq, tq)
        for h in range(MLA_HEADS):
            s = lax.dot_general(q_ref[:, hq(h)], k_ref[pl.ds(r0, tq), hq(h)],
                                (((1,), (1,)), ((), ())), preferred_element_type=F32) * c
            if masked:
                qc = lax.broadcasted_iota(jnp.int32, s.shape, 0) // CHUNK
                kc = lax.broadcasted_iota(jnp.int32, s.shape, 1) // CHUNK
                s = jnp.where(kc <= qc, s, -1e30)
            s_ref[h, j] = s
            mx_ref[h] = jnp.maximum(mx_ref[h], jnp.maximum(s[:, :LANES], s[:, LANES:]))

    def weighted(j):
        r0 = pl.multiple_of(j * tq, tq)
        for h in range(MLA_HEADS):
            m = mx_ref[h]
            p = jnp.exp2(s_ref[h, j] - jnp.concatenate([m, m], axis=1))
            l_ref[h] += p[:, :LANES] + p[:, LANES:]
            acc_ref[h] += _dot(p.astype(BF16), v_ref[pl.ds(r0, tq), hv(h)])

    def pairs(fn, n):
        def two(t, _):
            fn(2 * t)
            fn(2 * t + 1)
            return 0
        lax.fori_loop(0, n // 2, two, 0)

        @pl.when(n % 2 == 1)
        def _():
            fn(n - 1)

    pairs(lambda j: scores(j, False), i)
    scores(i, True)

    for h in range(MLA_HEADS):
        mx_ref[h] = jnp.broadcast_to(jnp.max(mx_ref[h], axis=-1, keepdims=True), (tq, LANES))
    l_ref[...] = jnp.zeros(l_ref.shape, F32)
    acc_ref[...] = jnp.zeros(acc_ref.shape, F32)

    pairs(weighted, i + 1)

    for h in range(MLA_HEADS):
        l = jnp.sum(l_ref[h], axis=-1, keepdims=True)
        o_ref[:, hv(h)] = (acc_ref[h] / l).astype(o_ref.dtype)


def _attention(q, k, v, w1, layer, batch, seq):
    m = q.shape[0]
    tq = ATTN_TQ
    nq = seq // tq
    f1 = D_FF // (batch * nq)
    step = lambda b, i: b * nq + i
    head_acc = pltpu.VMEM((MLA_HEADS, tq, LANES), F32)
    return pl.pallas_call(
        _attn_kernel,
        grid=(batch, nq),
        in_specs=[pl.BlockSpec((tq, 2 * MLA_WIDTH), lambda b, i: (step(b, i), 0)),
                  pl.BlockSpec((seq, 2 * MLA_WIDTH), lambda b, i: (b, 0)),
                  pl.BlockSpec((seq, MLA_WIDTH), lambda b, i: (b, 0)),
                  pl.BlockSpec((None, D_MODEL, f1), lambda b, i: (layer, 0, step(b, i)))],
        out_specs=[pl.BlockSpec((tq, MLA_WIDTH), lambda b, i: (step(b, i), 0)),
                   pl.BlockSpec((D_MODEL, f1), lambda b, i: (0, step(b, i)))],
        out_shape=[jax.ShapeDtypeStruct((m, MLA_WIDTH), BF16),
                   jax.ShapeDtypeStruct((D_MODEL, D_FF), BF16)],
        scratch_shapes=[pltpu.VMEM((MLA_HEADS, nq, tq, tq), F32), head_acc, head_acc, head_acc],
        compiler_params=pltpu.CompilerParams(
            dimension_semantics=("arbitrary", "arbitrary"), vmem_limit_bytes=58 * 1024 * 1024),
        name="attention",
    )(q, k, v, w1)


def _outproj_kernel(alpha, a_ref, c_ref, x_ref, w_ref, g_ref, b_ref, o_ref):
    for r in range(0, OUT_TM, OUT_TR):
        rows = slice(r, r + OUT_TR)
        mix = (_dot(a_ref[rows, :], w_ref[0:MLA_WIDTH, :])
               + _dot(c_ref[rows, :], w_ref[MLA_WIDTH:D_MODEL, :]))
        o_ref[rows, :] = _layer_norm(alpha * x_ref[rows, :] + mix, g_ref[...], b_ref[...])


def _outproj(attn, conv, x, w, g, b, alpha):
    m = x.shape[0]
    tm = OUT_TM
    row = lambda wd: pl.BlockSpec((tm, wd), lambda i: (i, 0))
    return pl.pallas_call(
        functools.partial(_outproj_kernel, alpha),
        grid=(m // tm,),
        in_specs=[row(MLA_WIDTH), row(CONV_WIDTH), row(D_MODEL), _const_spec(w.shape),
                  _const_spec(g.shape), _const_spec(b.shape)],
        out_specs=row(D_MODEL),
        out_shape=jax.ShapeDtypeStruct((m, D_MODEL), F32),
        compiler_params=pltpu.CompilerParams(
            dimension_semantics=("arbitrary",), vmem_limit_bytes=48 * 1024 * 1024),
        name="outproj",
    )(attn, conv, x, w, g, b)


def _mlp_kernel(alpha, x_ref, w1_ref, w2_ref, g_ref, b_ref, o_ref, xb_ref):
    f = pl.program_id(1)

    @pl.when(f == 0)
    def _():
        x = x_ref[...]
        xb_ref[...] = x.astype(BF16)
        o_ref[...] = alpha * x

    h = _dot(xb_ref[...], w1_ref[...])
    h = jnp.maximum(h, 0.0)
    hb = (h * h).astype(BF16)
    for n in range(0, D_MODEL, MLP_TN):
        o_ref[:, n:n + MLP_TN] += _dot(hb, w2_ref[:, n:n + MLP_TN])

    @pl.when(f == pl.num_programs(1) - 1)
    def _():
        o_ref[...] = _layer_norm(o_ref[...], g_ref[...], b_ref[...])


def _mlp(x, w1, w2, g, b, alpha):
    m = x.shape[0]
    tm, tf = MLP_TM, MLP_TF
    xspec = pl.BlockSpec((tm, D_MODEL), lambda i, f: (i, 0))
    return pl.pallas_call(
        functools.partial(_mlp_kernel, alpha),
        grid=(m // tm, D_FF // tf),
        in_specs=[xspec, pl.BlockSpec((D_MODEL, tf), lambda i, f: (0, f)),
                  pl.BlockSpec((tf, D_MODEL), lambda i, f: (f, 0)),
                  _const_spec(g.shape), _const_spec(b.shape)],
        out_specs=xspec,
        out_shape=jax.ShapeDtypeStruct((m, D_MODEL), F32),
        scratch_shapes=[pltpu.VMEM((tm, D_MODEL), BF16)],
        compiler_params=pltpu.CompilerParams(
            dimension_semantics=("arbitrary", "arbitrary"), vmem_limit_bytes=58 * 1024 * 1024),
        name="mlp",
    )(x, w1, w2, g, b)


def _swap_halves(w):
    return jnp.concatenate([-w[..., HALF_ROPE:], w[..., :HALF_ROPE]], axis=-1)


def _layer_weights(w_in, w_uq, w_ukv):
    o1, o2, o3 = Q_LORA, Q_LORA + KV_LORA, Q_LORA + KV_LORA + QK_ROPE
    w_kr = w_in[:, o2:o3]
    wa = jnp.concatenate([w_in[:, :o2], w_kr, _swap_halves(w_kr)], axis=1).astype(BF16)
    wu = w_in[:, o3:].astype(BF16)

    wq3 = w_uq.reshape(Q_LORA, MLA_HEADS, QK_NOPE + QK_ROPE)
    rope = wq3[:, :, QK_NOPE:]
    pad = jnp.zeros((Q_LORA, MLA_HEADS, LANES - QK_ROPE), w_uq.dtype)
    wq = jnp.concatenate([
        wq3[:, :, :QK_NOPE].reshape(Q_LORA, MLA_WIDTH),
        jnp.concatenate([rope, pad], axis=-1).reshape(Q_LORA, MLA_HEADS * LANES),
        jnp.concatenate([_swap_halves(rope), pad], axis=-1).reshape(Q_LORA, MLA_HEADS * LANES),
    ], axis=1).astype(BF16)

    wkv3 = w_ukv.reshape(KV_LORA, MLA_HEADS, QK_NOPE + V_HEAD)
    wkv = jnp.concatenate([wkv3[:, :, :QK_NOPE].reshape(KV_LORA, MLA_WIDTH),
                           wkv3[:, :, QK_NOPE:].reshape(KV_LORA, MLA_WIDTH)], axis=1).astype(BF16)
    return wa, wu, wq, wkv


def kernel(x, positions, ln_in_g, ln_in_b, w_in, g_q, w_uq, g_kv, w_ukv, b_glu, w_dw, b_dw, g_cln,
           b_cln, w_out, ln1_g, ln1_b, w1, w2, ln2_g, ln2_b):
    batch, seq, d = x.shape
    depth = w_in.shape[0]
    m = batch * seq
    alpha = (2.0 * depth) ** 0.25

    inv_freq = ROPE_THETA ** (-jnp.arange(0, QK_ROPE, 2, dtype=F32) / QK_ROPE)
    invf = jnp.tile(inv_freq, LANES // HALF_ROPE).reshape(1, LANES)
    pos = positions.reshape(m, 1)
    row = lambda a: a.reshape(1, -1)

    h = x.reshape(m, d)
    for l in range(depth):
        wa, wu, wq, wkv = _layer_weights(w_in[l], w_uq[l], w_ukv[l])
        ln = (row(ln_in_g), row(ln_in_b)) if l == 0 else None
        outs = _prep(h, pos, invf, ln, wa, wu, row(b_glu[l]), row(g_q[l]), wq, row(g_kv[l]), wkv,
                     w_dw[l], row(b_dw[l]), row(g_cln[l]), row(b_cln[l]), w2, w_out, l, seq)
        if l == 0:
            h, outs = outs[0], outs[1:]
        q, k, v, conv, w2_b, wo_b = outs
        attn, w1_b = _attention(q, k, v, w1, l, batch, seq)
        h = _outproj(attn, conv, h, wo_b, row(ln1_g[l]), row(ln1_b[l]), alpha)
        h = _mlp(h, w1_b, w2_b, row(ln2_g[l]), row(ln2_b[l]), alpha)
    return h.reshape(batch, seq, d)
```

```python
import functools

import jax
import jax.numpy as jnp
from jax import lax
from jax.experimental import pallas as pl
from jax.experimental.pallas import tpu as pltpu

D_MODEL = 2048
CHUNK = 64
MLA_HEADS = 8
QK_NOPE = 128
QK_ROPE = 64
V_HEAD = 128
Q_LORA = 512
KV_LORA = 256
MLA_WIDTH = MLA_HEADS * V_HEAD
CONV_WIDTH = D_MODEL - MLA_WIDTH
CONV_K = 31
D_FF = 4 * D_MODEL
ROPE_THETA = 10000.0
LN_EPS = 1e-5
RMS_EPS = 1e-6

LANES = 128
SUBLANES = 8
HALF_VREG = LANES // 2
HALF_ROPE = QK_ROPE // 2
W_IN_TR = 256
A_COLS = Q_LORA + KV_LORA + 2 * QK_ROPE
CONV_HALO = 32

PREP_TM = 256
PREP_CG = 256
ATTN_TQ = 256
CONV_TR = 32
OUT_TM = 1024
OUT_TR = 256
MLP_TM = 1024
MLP_TF = 512
MLP_TN = 512

BF16 = jnp.bfloat16
F32 = jnp.float32


def _layer_norm(x, g, b):
    mu = jnp.mean(x, axis=-1, keepdims=True)
    xc = x - mu
    var = jnp.mean(xc * xc, axis=-1, keepdims=True)
    return xc * lax.rsqrt(var + LN_EPS) * g + b


def _rms_norm(x, g):
    return x * lax.rsqrt(jnp.mean(x * x, axis=-1, keepdims=True) + RMS_EPS) * g


def _dot(a, b):
    return jnp.dot(a, b, preferred_element_type=F32)


def _const_spec(shape):
    return pl.BlockSpec(shape, lambda *_: (0,) * len(shape))


def _conv_rows(win_ref, r0, tr, c0, c1, w_ref, b_ref):
    lead = CONV_HALO - (CONV_K - 1)
    blocks = []
    for c in range(c0, c1, LANES):
        slab = win_ref[r0:r0 + tr + CONV_HALO, c:c + LANES]
        y = jnp.broadcast_to(b_ref[:, c:c + LANES], (tr, LANES))
        for b in range(SUBLANES):
            z = None
            nz = tr + (SUBLANES if b else 0)
            for a in range((lead + CONV_K - 1) // SUBLANES + 1):
                k = SUBLANES * a + b - lead
                if 0 <= k < CONV_K:
                    term = slab[SUBLANES * a:SUBLANES * a + nz, :] * w_ref[k:k + 1, c:c + LANES]
                    z = term if z is None else z + term
            y = y + z[b:b + tr, :]
        blocks.append(y)
    return jnp.concatenate(blocks, axis=1)


def _prep_kernel(apply_ln, tiles_per_seq, *refs):
    if apply_ln:
        (x_ref, pos_ref, invf_ref, lng_ref, lnb_ref, wa_ref, wu_ref, bglu_ref, gq_ref, wq_ref,
         gkv_ref, wkv_ref, wdw_ref, bdw_ref, gcln_ref, bcln_ref, w2_ref, wo_ref,
         h_ref, q_ref, k_ref, v_ref, conv_ref, w2b_ref, wob_ref, win_ref, pre_ref) = refs
    else:
        (x_ref, pos_ref, invf_ref, wa_ref, wu_ref, bglu_ref, gq_ref, wq_ref,
         gkv_ref, wkv_ref, wdw_ref, bdw_ref, gcln_ref, bcln_ref, w2_ref, wo_ref,
         q_ref, k_ref, v_ref, conv_ref, w2b_ref, wob_ref, win_ref, pre_ref) = refs
    tm = PREP_TM

    @pl.when(pl.program_id(0) % tiles_per_seq == 0)
    def _():
        win_ref[0:CONV_HALO, :] = jnp.zeros((CONV_HALO, CONV_WIDTH), F32)

    w2b_ref[...] = w2_ref[...].astype(BF16)
    wob_ref[...] = wo_ref[...].astype(BF16)

    nope = lambda h: slice(2 * h * LANES, (2 * h + 1) * LANES)
    rope = lambda h: slice((2 * h + 1) * LANES, (2 * h + 2) * LANES)
    x = x_ref[...]
    if apply_ln:
        x = _layer_norm(x, lng_ref[...], lnb_ref[...])
        h_ref[...] = x
    xb = x.astype(BF16)

    for c0 in range(0, CONV_WIDTH, PREP_CG):
        c1 = c0 + PREP_CG
        ua = _dot(xb, wu_ref[:, c0:c1]) + bglu_ref[:, c0:c1]
        ug = _dot(xb, wu_ref[:, CONV_WIDTH + c0:CONV_WIDTH + c1]) + bglu_ref[:, CONV_WIDTH + c0:CONV_WIDTH + c1]
        win_ref[CONV_HALO:CONV_HALO + tm, c0:c1] = ua * jax.nn.sigmoid(ug)
        for r0 in range(0, tm, CONV_TR):
            pre_ref[r0:r0 + CONV_TR, c0:c1] = _conv_rows(win_ref, r0, CONV_TR, c0, c1, wdw_ref, bdw_ref)
    for r0 in range(0, tm, CONV_TR):
        y = _layer_norm(pre_ref[r0:r0 + CONV_TR, :], gcln_ref[...], bcln_ref[...])
        conv_ref[r0:r0 + CONV_TR, :] = (y * jax.nn.sigmoid(y)).astype(BF16)

    ang = pos_ref[...].astype(F32) * invf_ref[...]
    cos_t = jnp.cos(ang)
    sin_t = jnp.sin(ang)

    za = _dot(xb, wa_ref[...])
    cq = za[:, :Q_LORA]
    ckv = za[:, Q_LORA:Q_LORA + KV_LORA]
    krb = za[:, Q_LORA + KV_LORA:]

    lane = lax.broadcasted_iota(jnp.int32, krb.shape, 1)
    t = krb * jnp.where(lane < QK_ROPE, cos_t, sin_t)
    kro = t + pltpu.roll(t, QK_ROPE, 1)
    kr = jnp.where(lane < QK_ROPE, kro, 0.0).astype(BF16)

    q = _dot(_rms_norm(cq, gq_ref[...]).astype(BF16), wq_ref[...])
    kv = _dot(_rms_norm(ckv, gkv_ref[...]).astype(BF16), wkv_ref[...])
    for h in range(MLA_HEADS):
        lo = MLA_WIDTH + h * LANES
        sw = 2 * MLA_WIDTH + h * LANES
        q_ref[:, nope(h)] = q[:, h * LANES:(h + 1) * LANES].astype(BF16)
        q_ref[:, rope(h)] = (q[:, lo:lo + LANES] * cos_t + q[:, sw:sw + LANES] * sin_t).astype(BF16)
        k_ref[:, nope(h)] = kv[:, h * LANES:(h + 1) * LANES].astype(BF16)
        k_ref[:, rope(h)] = kr
    v_ref[...] = kv[:, MLA_WIDTH:].astype(BF16)

    win_ref[0:CONV_HALO, :] = win_ref[tm:tm + CONV_HALO, :]


def _prep(x, pos, invf, ln, wa, wu, bglu, gq, wq, gkv, wkv, wdw, bdw, gcln, bcln, w2, w_out, layer,
          seq):
    m = x.shape[0]
    tm = PREP_TM
    steps = m // tm
    f2, fo = D_FF // steps, D_MODEL // steps
    apply_ln = ln is not None
    row = lambda w: pl.BlockSpec((tm, w), lambda i: (i, 0))
    in_specs = [row(D_MODEL), row(1), _const_spec((1, LANES))]
    args = [x, pos, invf]
    if apply_ln:
        in_specs += [_const_spec((1, D_MODEL))] * 2
        args += list(ln)
    slab = lambda a: pl.BlockSpec((None,) + a.shape[1:], lambda i: (layer, 0, 0))
    consts = [bglu, gq, wq, gkv, wkv, wdw, bdw, gcln, bcln]
    in_specs += [slab(wa), slab(wu)] + [_const_spec(a.shape) for a in consts]
    in_specs += [pl.BlockSpec((None, f2, D_MODEL), lambda i: (layer, i, 0)),
                 pl.BlockSpec((None, fo, D_MODEL), lambda i: (layer, i, 0))]
    args += [wa, wu] + consts + [w2, w_out]
    out_shape, out_specs = [], []
    if apply_ln:
        out_shape.append(jax.ShapeDtypeStruct((m, D_MODEL), F32))
        out_specs.append(row(D_MODEL))
    for width in (2 * MLA_WIDTH, 2 * MLA_WIDTH, MLA_WIDTH, CONV_WIDTH):
        out_shape.append(jax.ShapeDtypeStruct((m, width), BF16))
        out_specs.append(row(width))
    out_shape += [jax.ShapeDtypeStruct((D_FF, D_MODEL), BF16),
                  jax.ShapeDtypeStruct((D_MODEL, D_MODEL), BF16)]
    out_specs += [pl.BlockSpec((f2, D_MODEL), lambda i: (i, 0)),
                  pl.BlockSpec((fo, D_MODEL), lambda i: (i, 0))]
    return pl.pallas_call(
        functools.partial(_prep_kernel, apply_ln, seq // tm),
        grid=(steps,),
        in_specs=in_specs,
        out_specs=out_specs,
        out_shape=out_shape,
        scratch_shapes=[pltpu.VMEM((tm + CONV_HALO, CONV_WIDTH), F32),
                        pltpu.VMEM((tm, CONV_WIDTH), F32)],
        compiler_params=pltpu.CompilerParams(
            dimension_semantics=("arbitrary",), vmem_limit_bytes=58 * 1024 * 1024),
        name="prep_ln" if apply_ln else "prep",
    )(*args)


def _attn_kernel(q_ref, k_ref, v_ref, w1_ref, o_ref, w1b_ref, s_ref, mx_ref, l_ref, acc_ref):
    w1b_ref[...] = w1_ref[...].astype(BF16)

    i = pl.program_id(1)
    tq = ATTN_TQ
    c = (QK_NOPE + QK_ROPE) ** -0.5 * 1.4426950408889634
    hq = lambda h: slice(2 * h * LANES, 2 * (h + 1) * LANES)
    hv = lambda h: slice(h * LANES, (h + 1) * LANES)

    mx_ref[...] = jnp.full(mx_ref.shape, -jnp.inf, F32)

    def scores(j, masked):
        r0 = pl.multiple_of(j * tq, tq)
        for h in range(MLA_HEADS):
            s = lax.dot_general(q_ref[:, hq(h)], k_ref[pl.ds(r0, tq), hq(h)],
                                (((1,), (1,)), ((), ())), preferred_element_type=F32) * c
            if masked:
                qc = lax.broadcasted_iota(jnp.int32, s.shape, 0) // CHUNK
                kc = lax.broadcasted_iota(jnp.int32, s.shape, 1) // CHUNK
                s = jnp.where(kc <= qc, s, -1e30)
            s_ref[h, j] = s
            mx_ref[h] = jnp.maximum(mx_ref[h], jnp.maximum(s[:, :LANES], s[:, LANES:]))

    def weighted(j):
        r0 = pl.multiple_of(j * tq, tq)
        for h in range(MLA_HEADS):
            m = mx_ref[h]
            p = jnp.exp2(s_ref[h, j] - jnp.concatenate([m, m], axis=1))
            l_ref[h] += p[:, :LANES] + p[:, LANES:]
            acc_ref[h] += _dot(p.astype(BF16), v_ref[pl.ds(r0, tq), hv(h)])

    def pairs(fn, n):
        def two(t, _):
            fn(2 * t)
            fn(2 * t + 1)
            return 0
        lax.fori_loop(0, n // 2, two, 0)

        @pl.when(n % 2 == 1)
        def _():
            fn(n - 1)

    pairs(lambda j: scores(j, False), i)
    scores(i, True)

    for h in range(MLA_HEADS):
        mx_ref[h] = jnp.broadcast_to(jnp.max(mx_ref[h], axis=-1, keepdims=True), (tq, LANES))
    l_ref[...] = jnp.zeros(l_ref.shape, F32)
    acc_ref[...] = jnp.zeros(acc_ref.shape, F32)

    pairs(weighted, i + 1)

    for h in range(MLA_HEADS):
        l = jnp.sum(l_ref[h], axis=-1, keepdims=True)
        o_ref[:, hv(h)] = (acc_ref[h] / l).astype(o_ref.dtype)


def _attention(q, k, v, w1, layer, batch, seq):
    m = q.shape[0]
    tq = ATTN_TQ
    nq = seq // tq
    f1 = D_FF // (batch * nq)
    step = lambda b, i: b * nq + i
    head_acc = pltpu.VMEM((MLA_HEADS, tq, LANES), F32)
    return pl.pallas_call(
        _attn_kernel,
        grid=(batch, nq),
        in_specs=[pl.BlockSpec((tq, 2 * MLA_WIDTH), lambda b, i: (step(b, i), 0)),
                  pl.BlockSpec((seq, 2 * MLA_WIDTH), lambda b, i: (b, 0)),
                  pl.BlockSpec((seq, MLA_WIDTH), lambda b, i: (b, 0)),
                  pl.BlockSpec((None, D_MODEL, f1), lambda b, i: (layer, 0, step(b, i)))],
        out_specs=[pl.BlockSpec((tq, MLA_WIDTH), lambda b, i: (step(b, i), 0)),
                   pl.BlockSpec((D_MODEL, f1), lambda b, i: (0, step(b, i)))],
        out_shape=[jax.ShapeDtypeStruct((m, MLA_WIDTH), BF16),
                   jax.ShapeDtypeStruct((D_MODEL, D_FF), BF16)],
        scratch_shapes=[pltpu.VMEM((MLA_HEADS, nq, tq, tq), F32), head_acc, head_acc, head_acc],
        compiler_params=pltpu.CompilerParams(
            dimension_semantics=("arbitrary", "arbitrary"), vmem_limit_bytes=58 * 1024 * 1024),
        name="attention",
    )(q, k, v, w1)


def _outproj_kernel(alpha, a_ref, c_ref, x_ref, w_ref, g_ref, b_ref, o_ref):
    for r in range(0, OUT_TM, OUT_TR):
        rows = slice(r, r + OUT_TR)
        mix = (_dot(a_ref[rows, :], w_ref[0:MLA_WIDTH, :])
               + _dot(c_ref[rows, :], w_ref[MLA_WIDTH:D_MODEL, :]))
        o_ref[rows, :] = _layer_norm(alpha * x_ref[rows, :] + mix, g_ref[...], b_ref[...])


def _outproj(attn, conv, x, w, g, b, alpha):
    m = x.shape[0]
    tm = OUT_TM
    row = lambda wd: pl.BlockSpec((tm, wd), lambda i: (i, 0))
    return pl.pallas_call(
        functools.partial(_outproj_kernel, alpha),
        grid=(m // tm,),
        in_specs=[row(MLA_WIDTH), row(CONV_WIDTH), row(D_MODEL),
                  pl.BlockSpec(w.shape, lambda i: (0, 0), pipeline_mode=pl.Buffered(1)),
                  _const_spec(g.shape), _const_spec(b.shape)],
        out_specs=row(D_MODEL),
        out_shape=jax.ShapeDtypeStruct((m, D_MODEL), F32),
        compiler_params=pltpu.CompilerParams(
            dimension_semantics=("arbitrary",), vmem_limit_bytes=56 * 1024 * 1024),
        name="outproj",
    )(attn, conv, x, w, g, b)


def _mlp_kernel(alpha, x_ref, w1_ref, w2_ref, g_ref, b_ref, o_ref, xb_ref):
    f = pl.program_id(1)

    @pl.when(f == 0)
    def _():
        x = x_ref[...]
        xb_ref[...] = x.astype(BF16)
        o_ref[...] = alpha * x

    h = _dot(xb_ref[...], w1_ref[...])
    h = jnp.maximum(h, 0.0)
    hb = (h * h).astype(BF16)
    for n in range(0, D_MODEL, MLP_TN):
        o_ref[:, n:n + MLP_TN] += _dot(hb, w2_ref[:, n:n + MLP_TN])

    @pl.when(f == pl.num_programs(1) - 1)
    def _():
        o_ref[...] = _layer_norm(o_ref[...], g_ref[...], b_ref[...])


def _mlp(x, w1, w2, g, b, alpha):
    m = x.shape[0]
    tm, tf = MLP_TM, MLP_TF
    xspec = pl.BlockSpec((tm, D_MODEL), lambda i, f: (i, 0))
    return pl.pallas_call(
        functools.partial(_mlp_kernel, alpha),
        grid=(m // tm, D_FF // tf),
        in_specs=[xspec, pl.BlockSpec((D_MODEL, tf), lambda i, f: (0, f)),
                  pl.BlockSpec((tf, D_MODEL), lambda i, f: (f, 0)),
                  _const_spec(g.shape), _const_spec(b.shape)],
        out_specs=xspec,
        out_shape=jax.ShapeDtypeStruct((m, D_MODEL), F32),
        scratch_shapes=[pltpu.VMEM((tm, D_MODEL), BF16)],
        compiler_params=pltpu.CompilerParams(
            dimension_semantics=("arbitrary", "arbitrary"), vmem_limit_bytes=58 * 1024 * 1024),
        name="mlp",
    )(x, w1, w2, g, b)


def _swap_halves(w):
    return jnp.concatenate([-w[..., HALF_ROPE:], w[..., :HALF_ROPE]], axis=-1)


def _w_in_kernel(main_ref, tail_ref, wa_ref, wu_ref):
    kv_end = Q_LORA + KV_LORA
    lane = lax.broadcasted_iota(jnp.int32, (main_ref.shape[0], LANES), 1)
    wa_ref[:, :kv_end] = main_ref[:, :kv_end].astype(BF16)
    ncol = D_MODEL // LANES
    cols = [main_ref[:, kv_end + j * LANES:kv_end + (j + 1) * LANES] for j in range(ncol)]
    cols.append(tail_ref[...])
    c = cols[0]
    partner = jnp.where(lane < QK_ROPE + HALF_ROPE, -pltpu.roll(c, HALF_ROPE, 1),
                        pltpu.roll(c, QK_ROPE + HALF_ROPE, 1))
    wa_ref[:, kv_end:] = jnp.where(lane < QK_ROPE, c, partner).astype(BF16)
    rolled = [pltpu.roll(c, HALF_VREG, 1) for c in cols]
    for j in range(ncol):
        wu_ref[:, j * LANES:(j + 1) * LANES] = jnp.where(
            lane < HALF_VREG, rolled[j], rolled[j + 1]).astype(BF16)


def _w_in_layout(w_in):
    depth, d, in_cols = w_in.shape
    main_cols = in_cols - HALF_VREG
    tail = jnp.pad(w_in[:, :, main_cols:], ((0, 0), (0, 0), (0, LANES - HALF_VREG)))
    tr = W_IN_TR
    return pl.pallas_call(
        _w_in_kernel,
        grid=(depth, d // tr),
        in_specs=[pl.BlockSpec((None, tr, main_cols), lambda l, i: (l, i, 0)),
                  pl.BlockSpec((None, tr, LANES), lambda l, i: (l, i, 0))],
        out_specs=[pl.BlockSpec((None, tr, A_COLS), lambda l, i: (l, i, 0)),
                   pl.BlockSpec((None, tr, 2 * CONV_WIDTH), lambda l, i: (l, i, 0))],
        out_shape=[jax.ShapeDtypeStruct((depth, d, A_COLS), BF16),
                   jax.ShapeDtypeStruct((depth, d, 2 * CONV_WIDTH), BF16)],
        compiler_params=pltpu.CompilerParams(dimension_semantics=("arbitrary", "arbitrary")),
        name="w_in_layout",
    )(w_in, tail)


def _up_weights(w_uq, w_ukv):
    wq3 = w_uq.reshape(Q_LORA, MLA_HEADS, QK_NOPE + QK_ROPE)
    rope = wq3[:, :, QK_NOPE:]
    pad = jnp.zeros((Q_LORA, MLA_HEADS, LANES - QK_ROPE), w_uq.dtype)
    wq = jnp.concatenate([
        wq3[:, :, :QK_NOPE].reshape(Q_LORA, MLA_WIDTH),
        jnp.concatenate([rope, pad], axis=-1).reshape(Q_LORA, MLA_HEADS * LANES),
        jnp.concatenate([_swap_halves(rope), pad], axis=-1).reshape(Q_LORA, MLA_HEADS * LANES),
    ], axis=1).astype(BF16)

    wkv3 = w_ukv.reshape(KV_LORA, MLA_HEADS, QK_NOPE + V_HEAD)
    wkv = jnp.concatenate([wkv3[:, :, :QK_NOPE].reshape(KV_LORA, MLA_WIDTH),
                           wkv3[:, :, QK_NOPE:].reshape(KV_LORA, MLA_WIDTH)], axis=1).astype(BF16)
    return wq, wkv


def kernel(x, positions, ln_in_g, ln_in_b, w_in, g_q, w_uq, g_kv, w_ukv, b_glu, w_dw, b_dw, g_cln,
           b_cln, w_out, ln1_g, ln1_b, w1, w2, ln2_g, ln2_b):
    batch, seq, d = x.shape
    depth = w_in.shape[0]
    m = batch * seq
    alpha = (2.0 * depth) ** 0.25

    inv_freq = ROPE_THETA ** (-jnp.arange(0, QK_ROPE, 2, dtype=F32) / QK_ROPE)
    invf = jnp.tile(inv_freq, LANES // HALF_ROPE).reshape(1, LANES)
    pos = positions.reshape(m, 1)
    row = lambda a: a.reshape(1, -1)

    wa, wu = _w_in_layout(w_in)
    h = x.reshape(m, d)
    for l in range(depth):
        wq, wkv = _up_weights(w_uq[l], w_ukv[l])
        ln = (row(ln_in_g), row(ln_in_b)) if l == 0 else None
        outs = _prep(h, pos, invf, ln, wa, wu, row(b_glu[l]), row(g_q[l]), wq, row(g_kv[l]), wkv,
                     w_dw[l], row(b_dw[l]), row(g_cln[l]), row(b_cln[l]), w2, w_out, l, seq)
        if l == 0:
            h, outs = outs[0], outs[1:]
        q, k, v, conv, w2_b, wo_b = outs
        attn, w1_b = _attention(q, k, v, w1, l, batch, seq)
        h = _outproj(attn, conv, h, wo_b, row(ln1_g[l]), row(ln1_b[l]), alpha)
        h = _mlp(h, w1_b, w2_b, row(ln2_g[l]), row(ln2_b[l]), alpha)
    return h.reshape(batch, seq, d)
```

```python
import functools

import jax
import jax.numpy as jnp
from jax import lax
from jax.experimental import pallas as pl
from jax.experimental.pallas import tpu as pltpu

D_MODEL = 2048
CHUNK = 64
MLA_HEADS = 8
QK_NOPE = 128
QK_ROPE = 64
V_HEAD = 128
Q_LORA = 512
KV_LORA = 256
MLA_WIDTH = MLA_HEADS * V_HEAD
CONV_WIDTH = D_MODEL - MLA_WIDTH
CONV_K = 31
D_FF = 4 * D_MODEL
ROPE_THETA = 10000.0
LN_EPS = 1e-5
RMS_EPS = 1e-6

LANES = 128
SUBLANES = 8
HALF_ROPE = QK_ROPE // 2
W_IN_TC = 256
A_COLS = Q_LORA + KV_LORA + 2 * QK_ROPE
CONV_HALO = 32

PREP_TM = 256
PREP_CG = 256
ATTN_TQ = 256
CONV_TR = 32
OUT_TM = 512
OUT_TR = 128
MLP_TM = 1024
MLP_TF = 512
MLP_TN = 512

BF16 = jnp.bfloat16
F32 = jnp.float32


def _layer_norm(x, g, b):
    mu = jnp.mean(x, axis=-1, keepdims=True)
    xc = x - mu
    var = jnp.mean(xc * xc, axis=-1, keepdims=True)
    return xc * lax.rsqrt(var + LN_EPS) * g + b


def _rms_norm(x, g):
    return x * lax.rsqrt(jnp.mean(x * x, axis=-1, keepdims=True) + RMS_EPS) * g


def _dot(a, b):
    return jnp.dot(a, b, preferred_element_type=F32)


def _const_spec(shape):
    return pl.BlockSpec(shape, lambda *_: (0,) * len(shape))


def _conv_rows(win_ref, r0, tr, c0, c1, w_ref, b_ref):
    lead = CONV_HALO - (CONV_K - 1)
    blocks = []
    for c in range(c0, c1, LANES):
        slab = win_ref[r0:r0 + tr + CONV_HALO, c:c + LANES]
        y = jnp.broadcast_to(b_ref[:, c:c + LANES], (tr, LANES))
        for b in range(SUBLANES):
            z = None
            nz = tr + (SUBLANES if b else 0)
            for a in range((lead + CONV_K - 1) // SUBLANES + 1):
                k = SUBLANES * a + b - lead
                if 0 <= k < CONV_K:
                    term = slab[SUBLANES * a:SUBLANES * a + nz, :] * w_ref[k:k + 1, c:c + LANES]
                    z = term if z is None else z + term
            y = y + z[b:b + tr, :]
        blocks.append(y)
    return jnp.concatenate(blocks, axis=1)


def _prep_kernel(apply_ln, tiles_per_seq, *refs):
    if apply_ln:
        (x_ref, pos_ref, invf_ref, lng_ref, lnb_ref, wa_ref, wu_ref, bglu_ref, gq_ref, wq_ref,
         gkv_ref, wkv_ref, wdw_ref, bdw_ref, gcln_ref, bcln_ref, w2_ref, wo_ref,
         h_ref, q_ref, k_ref, v_ref, conv_ref, w2b_ref, wob_ref, win_ref, pre_ref) = refs
    else:
        (x_ref, pos_ref, invf_ref, wa_ref, wu_ref, bglu_ref, gq_ref, wq_ref,
         gkv_ref, wkv_ref, wdw_ref, bdw_ref, gcln_ref, bcln_ref, w2_ref, wo_ref,
         q_ref, k_ref, v_ref, conv_ref, w2b_ref, wob_ref, win_ref, pre_ref) = refs
    tm = PREP_TM

    @pl.when(pl.program_id(0) % tiles_per_seq == 0)
    def _():
        win_ref[0:CONV_HALO, :] = jnp.zeros((CONV_HALO, CONV_WIDTH), F32)

    w2b_ref[...] = w2_ref[...].astype(BF16)
    wob_ref[...] = wo_ref[...].astype(BF16)

    nope = lambda h: slice(2 * h * LANES, (2 * h + 1) * LANES)
    rope = lambda h: slice((2 * h + 1) * LANES, (2 * h + 2) * LANES)
    x = x_ref[...]
    if apply_ln:
        x = _layer_norm(x, lng_ref[...], lnb_ref[...])
        h_ref[...] = x
    xb = x.astype(BF16)

    for c0 in range(0, CONV_WIDTH, PREP_CG):
        c1 = c0 + PREP_CG
        ua = _dot(xb, wu_ref[:, c0:c1]) + bglu_ref[:, c0:c1]
        ug = _dot(xb, wu_ref[:, CONV_WIDTH + c0:CONV_WIDTH + c1]) + bglu_ref[:, CONV_WIDTH + c0:CONV_WIDTH + c1]
        win_ref[CONV_HALO:CONV_HALO + tm, c0:c1] = ua * jax.nn.sigmoid(ug)
        for r0 in range(0, tm, CONV_TR):
            pre_ref[r0:r0 + CONV_TR, c0:c1] = _conv_rows(win_ref, r0, CONV_TR, c0, c1, wdw_ref, bdw_ref)
    for r0 in range(0, tm, CONV_TR):
        y = _layer_norm(pre_ref[r0:r0 + CONV_TR, :], gcln_ref[...], bcln_ref[...])
        conv_ref[r0:r0 + CONV_TR, :] = (y * jax.nn.sigmoid(y)).astype(BF16)

    ang = pos_ref[...].astype(F32) * invf_ref[...]
    cos_t = jnp.cos(ang)
    sin_t = jnp.sin(ang)

    za = _dot(xb, wa_ref[...])
    cq = za[:, :Q_LORA]
    ckv = za[:, Q_LORA:Q_LORA + KV_LORA]
    krb = za[:, Q_LORA + KV_LORA:]

    lane = lax.broadcasted_iota(jnp.int32, krb.shape, 1)
    t = krb * jnp.where(lane < QK_ROPE, cos_t, sin_t)
    kro = t + pltpu.roll(t, QK_ROPE, 1)
    kr = jnp.where(lane < QK_ROPE, kro, 0.0).astype(BF16)

    q = _dot(_rms_norm(cq, gq_ref[...]).astype(BF16), wq_ref[...])
    kv = _dot(_rms_norm(ckv, gkv_ref[...]).astype(BF16), wkv_ref[...])
    for h in range(MLA_HEADS):
        lo = MLA_WIDTH + h * LANES
        sw = 2 * MLA_WIDTH + h * LANES
        q_ref[:, nope(h)] = q[:, h * LANES:(h + 1) * LANES].astype(BF16)
        q_ref[:, rope(h)] = (q[:, lo:lo + LANES] * cos_t + q[:, sw:sw + LANES] * sin_t).astype(BF16)
        k_ref[:, nope(h)] = kv[:, h * LANES:(h + 1) * LANES].astype(BF16)
        k_ref[:, rope(h)] = kr
    v_ref[...] = kv[:, MLA_WIDTH:].astype(BF16)

    win_ref[0:CONV_HALO, :] = win_ref[tm:tm + CONV_HALO, :]


def _prep(x, pos, invf, ln, wa, wu, bglu, gq, wq, gkv, wkv, wdw, bdw, gcln, bcln, w2, w_out, layer,
          seq):
    m = x.shape[0]
    tm = PREP_TM
    steps = m // tm
    f2, fo = D_FF // steps, D_MODEL // steps
    apply_ln = ln is not None
    row = lambda w: pl.BlockSpec((tm, w), lambda i: (i, 0))
    in_specs = [row(D_MODEL), row(1), _const_spec((1, LANES))]
    args = [x, pos, invf]
    if apply_ln:
        in_specs += [_const_spec((1, D_MODEL))] * 2
        args += list(ln)
    slab = lambda a: pl.BlockSpec((None,) + a.shape[1:], lambda i: (layer, 0, 0),
                                  pipeline_mode=pl.Buffered(1))
    once = lambda a: pl.BlockSpec(a.shape, lambda i: (0, 0), pipeline_mode=pl.Buffered(1))
    consts = [bglu, gq, wq, gkv, wkv, wdw, bdw, gcln, bcln]
    in_specs += [slab(wa), slab(wu)]
    in_specs += [once(a) if a is wq or a is wkv else _const_spec(a.shape) for a in consts]
    in_specs += [pl.BlockSpec((None, f2, D_MODEL), lambda i: (layer, i, 0)),
                 pl.BlockSpec((None, fo, D_MODEL), lambda i: (layer, i, 0))]
    args += [wa, wu] + consts + [w2, w_out]
    out_shape, out_specs = [], []
    if apply_ln:
        out_shape.append(jax.ShapeDtypeStruct((m, D_MODEL), F32))
        out_specs.append(row(D_MODEL))
    for width in (2 * MLA_WIDTH, 2 * MLA_WIDTH, MLA_WIDTH, CONV_WIDTH):
        out_shape.append(jax.ShapeDtypeStruct((m, width), BF16))
        out_specs.append(row(width))
    out_shape += [jax.ShapeDtypeStruct((D_FF, D_MODEL), BF16),
                  jax.ShapeDtypeStruct((D_MODEL, D_MODEL), BF16)]
    out_specs += [pl.BlockSpec((f2, D_MODEL), lambda i: (i, 0)),
                  pl.BlockSpec((fo, D_MODEL), lambda i: (i, 0))]
    return pl.pallas_call(
        functools.partial(_prep_kernel, apply_ln, seq // tm),
        grid=(steps,),
        in_specs=in_specs,
        out_specs=out_specs,
        out_shape=out_shape,
        scratch_shapes=[pltpu.VMEM((tm + CONV_HALO, CONV_WIDTH), F32),
                        pltpu.VMEM((tm, CONV_WIDTH), F32)],
        compiler_params=pltpu.CompilerParams(
            dimension_semantics=("arbitrary",), vmem_limit_bytes=40 * 1024 * 1024),
        name="prep_ln" if apply_ln else "prep",
    )(*args)


def _attn_kernel(q_ref, k_ref, v_ref, w1_ref, o_ref, w1b_ref, s_ref, mx_ref, l_ref, acc_ref):
    w1b_ref[...] = w1_ref[...].astype(BF16)

    i = pl.program_id(1)
    tq = ATTN_TQ
    c = (QK_NOPE + QK_ROPE) ** -0.5 * 1.4426950408889634
    hq = lambda h: slice(2 * h * LANES, 2 * (h + 1) * LANES)
    hv = lambda h: slice(h * LANES, (h + 1) * LANES)

    mx_ref[...] = jnp.full(mx_ref.shape, -jnp.inf, F32)

    def scores(j, masked):
        r0 = pl.multiple_of(j * tq, tq)
        for h in range(MLA_HEADS):
            s = lax.dot_general(q_ref[:, hq(h)], k_ref[pl.ds(r0, tq), hq(h)],
                                (((1,), (1,)), ((), ())), preferred_element_type=F32) * c
            if masked:
                qc = lax.broadcasted_iota(jnp.int32, s.shape, 0) // CHUNK
                kc = lax.broadcasted_iota(jnp.int32, s.shape, 1) // CHUNK
                s = jnp.where(kc <= qc, s, -1e30)
            s_ref[h, j] = s
            mx_ref[h] = jnp.maximum(mx_ref[h], jnp.maximum(s[:, :LANES], s[:, LANES:]))

    def weighted(j):
        r0 = pl.multiple_of(j * tq, tq)
        for h in range(MLA_HEADS):
            m = mx_ref[h]
            p = jnp.exp2(s_ref[h, j] - jnp.concatenate([m, m], axis=1))
            l_ref[h] += p[:, :LANES] + p[:, LANES:]
            acc_ref[h] += _dot(p.astype(BF16), v_ref[pl.ds(r0, tq), hv(h)])

    def pairs(fn, n):
        def two(t, _):
            fn(2 * t)
            fn(2 * t + 1)
            return 0
        lax.fori_loop(0, n // 2, two, 0)

        @pl.when(n % 2 == 1)
        def _():
            fn(n - 1)

    pairs(lambda j: scores(j, False), i)
    scores(i, True)

    for h in range(MLA_HEADS):
        mx_ref[h] = jnp.broadcast_to(jnp.max(mx_ref[h], axis=-1, keepdims=True), (tq, LANES))
    l_ref[...] = jnp.zeros(l_ref.shape, F32)
    acc_ref[...] = jnp.zeros(acc_ref.shape, F32)

    pairs(weighted, i + 1)

    for h in range(MLA_HEADS):
        l = jnp.sum(l_ref[h], axis=-1, keepdims=True)
        o_ref[:, hv(h)] = (acc_ref[h] / l).astype(o_ref.dtype)


def _attention(q, k, v, w1, layer, batch, seq):
    m = q.shape[0]
    tq = ATTN_TQ
    nq = seq // tq
    f1 = D_FF // (batch * nq)
    step = lambda b, i: b * nq + i
    head_acc = pltpu.VMEM((MLA_HEADS, tq, LANES), F32)
    return pl.pallas_call(
        _attn_kernel,
        grid=(batch, nq),
        in_specs=[pl.BlockSpec((tq, 2 * MLA_WIDTH), lambda b, i: (step(b, i), 0)),
                  pl.BlockSpec((seq, 2 * MLA_WIDTH), lambda b, i: (b, 0)),
                  pl.BlockSpec((seq, MLA_WIDTH), lambda b, i: (b, 0)),
                  pl.BlockSpec((None, D_MODEL, f1), lambda b, i: (layer, 0, step(b, i)))],
        out_specs=[pl.BlockSpec((tq, MLA_WIDTH), lambda b, i: (step(b, i), 0)),
                   pl.BlockSpec((D_MODEL, f1), lambda b, i: (0, step(b, i)))],
        out_shape=[jax.ShapeDtypeStruct((m, MLA_WIDTH), BF16),
                   jax.ShapeDtypeStruct((D_MODEL, D_FF), BF16)],
        scratch_shapes=[pltpu.VMEM((MLA_HEADS, nq, tq, tq), F32), head_acc, head_acc, head_acc],
        compiler_params=pltpu.CompilerParams(
            dimension_semantics=("arbitrary", "arbitrary"), vmem_limit_bytes=58 * 1024 * 1024),
        name="attention",
    )(q, k, v, w1)


def _outproj_kernel(alpha, a_ref, c_ref, x_ref, w_ref, g_ref, b_ref, o_ref):
    for r in range(0, OUT_TM, OUT_TR):
        rows = slice(r, r + OUT_TR)
        mix = (_dot(a_ref[rows, :], w_ref[0:MLA_WIDTH, :])
               + _dot(c_ref[rows, :], w_ref[MLA_WIDTH:D_MODEL, :]))
        o_ref[rows, :] = _layer_norm(alpha * x_ref[rows, :] + mix, g_ref[...], b_ref[...])


def _outproj(attn, conv, x, w, g, b, alpha):
    m = x.shape[0]
    tm = OUT_TM
    row = lambda wd: pl.BlockSpec((tm, wd), lambda i: (i, 0))
    return pl.pallas_call(
        functools.partial(_outproj_kernel, alpha),
        grid=(m // tm,),
        in_specs=[row(MLA_WIDTH), row(CONV_WIDTH), row(D_MODEL),
                  pl.BlockSpec(w.shape, lambda i: (0, 0), pipeline_mode=pl.Buffered(1)),
                  _const_spec(g.shape), _const_spec(b.shape)],
        out_specs=row(D_MODEL),
        out_shape=jax.ShapeDtypeStruct((m, D_MODEL), F32),
        compiler_params=pltpu.CompilerParams(
            dimension_semantics=("arbitrary",), vmem_limit_bytes=56 * 1024 * 1024),
        name="outproj",
    )(attn, conv, x, w, g, b)


def _mlp_kernel(alpha, x_ref, w1_ref, w2_ref, g_ref, b_ref, o_ref, xb_ref):
    f = pl.program_id(1)

    @pl.when(f == 0)
    def _():
        x = x_ref[...]
        xb_ref[...] = x.astype(BF16)
        o_ref[...] = alpha * x

    h = _dot(xb_ref[...], w1_ref[...])
    h = jnp.maximum(h, 0.0)
    hb = (h * h).astype(BF16)
    for n in range(0, D_MODEL, MLP_TN):
        o_ref[:, n:n + MLP_TN] += _dot(hb, w2_ref[:, n:n + MLP_TN])

    @pl.when(f == pl.num_programs(1) - 1)
    def _():
        o_ref[...] = _layer_norm(o_ref[...], g_ref[...], b_ref[...])


def _mlp(x, w1, w2, g, b, alpha):
    m = x.shape[0]
    tm, tf = MLP_TM, MLP_TF
    xspec = pl.BlockSpec((tm, D_MODEL), lambda i, f: (i, 0))
    return pl.pallas_call(
        functools.partial(_mlp_kernel, alpha),
        grid=(m // tm, D_FF // tf),
        in_specs=[xspec, pl.BlockSpec((D_MODEL, tf), lambda i, f: (0, f)),
                  pl.BlockSpec((tf, D_MODEL), lambda i, f: (f, 0)),
                  _const_spec(g.shape), _const_spec(b.shape)],
        out_specs=xspec,
        out_shape=jax.ShapeDtypeStruct((m, D_MODEL), F32),
        scratch_shapes=[pltpu.VMEM((tm, D_MODEL), BF16)],
        compiler_params=pltpu.CompilerParams(
            dimension_semantics=("arbitrary", "arbitrary"), vmem_limit_bytes=58 * 1024 * 1024),
        name="mlp",
    )(x, w1, w2, g, b)


def _swap_halves(w):
    return jnp.concatenate([-w[..., HALF_ROPE:], w[..., :HALF_ROPE]], axis=-1)


def _w_in_kernel(wt_ref, wa_ref, wu_ref):
    kv_end = Q_LORA + KV_LORA
    kr_end = kv_end + QK_ROPE
    wa_ref[:, :kv_end] = wt_ref[:kv_end, :].T.astype(BF16)
    k1 = wt_ref[kv_end:kv_end + HALF_ROPE, :]
    k2 = wt_ref[kv_end + HALF_ROPE:kr_end, :]
    wa_ref[:, kv_end:] = jnp.concatenate([k1, k2, -k2, k1], axis=0).T.astype(BF16)
    wu_ref[...] = wt_ref[kr_end:, :].T.astype(BF16)


def _w_in_layout(w_in):
    depth, d, in_cols = w_in.shape
    w_in_t = jnp.swapaxes(w_in, 1, 2)
    tc = W_IN_TC
    return pl.pallas_call(
        _w_in_kernel,
        grid=(depth, d // tc),
        in_specs=[pl.BlockSpec((None, in_cols, tc), lambda l, i: (l, 0, i))],
        out_specs=[pl.BlockSpec((None, tc, A_COLS), lambda l, i: (l, i, 0)),
                   pl.BlockSpec((None, tc, 2 * CONV_WIDTH), lambda l, i: (l, i, 0))],
        out_shape=[jax.ShapeDtypeStruct((depth, d, A_COLS), BF16),
                   jax.ShapeDtypeStruct((depth, d, 2 * CONV_WIDTH), BF16)],
        compiler_params=pltpu.CompilerParams(dimension_semantics=("arbitrary", "arbitrary")),
        name="w_in_layout",
    )(w_in_t)


def _up_weights(w_uq, w_ukv):
    wq3 = w_uq.reshape(Q_LORA, MLA_HEADS, QK_NOPE + QK_ROPE)
    rope = wq3[:, :, QK_NOPE:]
    pad = jnp.zeros((Q_LORA, MLA_HEADS, LANES - QK_ROPE), w_uq.dtype)
    wq = jnp.concatenate([
        wq3[:, :, :QK_NOPE].reshape(Q_LORA, MLA_WIDTH),
        jnp.concatenate([rope, pad], axis=-1).reshape(Q_LORA, MLA_HEADS * LANES),
        jnp.concatenate([_swap_halves(rope), pad], axis=-1).reshape(Q_LORA, MLA_HEADS * LANES),
    ], axis=1).astype(BF16)

    wkv3 = w_ukv.reshape(KV_LORA, MLA_HEADS, QK_NOPE + V_HEAD)
    wkv = jnp.concatenate([wkv3[:, :, :QK_NOPE].reshape(KV_LORA, MLA_WIDTH),
                           wkv3[:, :, QK_NOPE:].reshape(KV_LORA, MLA_WIDTH)], axis=1).astype(BF16)
    return wq, wkv


def kernel(x, positions, ln_in_g, ln_in_b, w_in, g_q, w_uq, g_kv, w_ukv, b_glu, w_dw, b_dw, g_cln,
           b_cln, w_out, ln1_g, ln1_b, w1, w2, ln2_g, ln2_b):
    batch, seq, d = x.shape
    depth = w_in.shape[0]
    m = batch * seq
    alpha = (2.0 * depth) ** 0.25

    inv_freq = ROPE_THETA ** (-jnp.arange(0, QK_ROPE, 2, dtype=F32) / QK_ROPE)
    invf = jnp.tile(inv_freq, LANES // HALF_ROPE).reshape(1, LANES)
    pos = positions.reshape(m, 1)
    row = lambda a: a.reshape(1, -1)

    wa, wu = _w_in_layout(w_in)
    h = x.reshape(m, d)
    for l in range(depth):
        wq, wkv = _up_weights(w_uq[l], w_ukv[l])
        ln = (row(ln_in_g), row(ln_in_b)) if l == 0 else None
        outs = _prep(h, pos, invf, ln, wa, wu, row(b_glu[l]), row(g_q[l]), wq, row(g_kv[l]), wkv,
                     w_dw[l], row(b_dw[l]), row(g_cln[l]), row(b_cln[l]), w2, w_out, l, seq)
        if l == 0:
            h, outs = outs[0], outs[1:]
        q, k, v, conv, w2_b, wo_b = outs
        attn, w1_b = _attention(q, k, v, w1, l, batch, seq)
        h = _outproj(attn, conv, h, wo_b, row(ln1_g[l]), row(ln1_b[l]), alpha)
        h = _mlp(h, w1_b, w2_b, row(ln2_g[l]), row(ln2_b[l]), alpha)
    return h.reshape(batch, seq, d)
```

```python
import functools

import jax
import jax.numpy as jnp
from jax import lax
from jax.experimental import pallas as pl
from jax.experimental.pallas import tpu as pltpu

D_MODEL = 2048
CHUNK = 64
MLA_HEADS = 8
QK_NOPE = 128
QK_ROPE = 64
V_HEAD = 128
Q_LORA = 512
KV_LORA = 256
MLA_WIDTH = MLA_HEADS * V_HEAD
CONV_WIDTH = D_MODEL - MLA_WIDTH
CONV_K = 31
D_FF = 4 * D_MODEL
ROPE_THETA = 10000.0
LN_EPS = 1e-5
RMS_EPS = 1e-6

LANES = 128
SUBLANES = 8
HALF_ROPE = QK_ROPE // 2
W_IN_TC = 256
A_COLS = Q_LORA + KV_LORA + 2 * QK_ROPE
CONV_HALO = 32

PREP_TM = 256
PREP_CG = 256
ATTN_TQ = 256
CONV_TR = 32
OUT_TM = 512
OUT_TR = 128
MLP_TM = 1024
MLP_TF = 512
MLP_TN = 512
MLP_TR = 256

BF16 = jnp.bfloat16
F32 = jnp.float32


def _layer_norm(x, g, b):
    mu = jnp.mean(x, axis=-1, keepdims=True)
    xc = x - mu
    var = jnp.mean(xc * xc, axis=-1, keepdims=True)
    return xc * lax.rsqrt(var + LN_EPS) * g + b


def _rms_norm(x, g):
    return x * lax.rsqrt(jnp.mean(x * x, axis=-1, keepdims=True) + RMS_EPS) * g


def _dot(a, b):
    return jnp.dot(a, b, preferred_element_type=F32)


def _const_spec(shape):
    return pl.BlockSpec(shape, lambda *_: (0,) * len(shape))


def _conv_rows(win_ref, r0, tr, c0, c1, w_ref, b_ref):
    lead = CONV_HALO - (CONV_K - 1)
    blocks = []
    for c in range(c0, c1, LANES):
        slab = win_ref[r0:r0 + tr + CONV_HALO, c:c + LANES]
        y = jnp.broadcast_to(b_ref[:, c:c + LANES], (tr, LANES))
        for b in range(SUBLANES):
            z = None
            nz = tr + (SUBLANES if b else 0)
            for a in range((lead + CONV_K - 1) // SUBLANES + 1):
                k = SUBLANES * a + b - lead
                if 0 <= k < CONV_K:
                    term = slab[SUBLANES * a:SUBLANES * a + nz, :] * w_ref[k:k + 1, c:c + LANES]
                    z = term if z is None else z + term
            y = y + z[b:b + tr, :]
        blocks.append(y)
    return jnp.concatenate(blocks, axis=1)


def _prep_kernel(apply_ln, tiles_per_seq, *refs):
    if apply_ln:
        (x_ref, pos_ref, invf_ref, lng_ref, lnb_ref, wa_ref, wu_ref, bglu_ref, gq_ref, wq_ref,
         gkv_ref, wkv_ref, wdw_ref, bdw_ref, gcln_ref, bcln_ref, w2_ref, wo_ref,
         h_ref, cos_ref, sin_ref, q_ref, k_ref, v_ref, conv_ref, w2b_ref, wob_ref,
         win_ref, pre_ref) = refs
    else:
        (x_ref, cos_ref, sin_ref, wa_ref, wu_ref, bglu_ref, gq_ref, wq_ref,
         gkv_ref, wkv_ref, wdw_ref, bdw_ref, gcln_ref, bcln_ref, w2_ref, wo_ref,
         q_ref, k_ref, v_ref, conv_ref, w2b_ref, wob_ref, win_ref, pre_ref) = refs
    tm = PREP_TM

    @pl.when(pl.program_id(0) % tiles_per_seq == 0)
    def _():
        win_ref[0:CONV_HALO, :] = jnp.zeros((CONV_HALO, CONV_WIDTH), F32)

    w2b_ref[...] = w2_ref[...].astype(BF16)
    wob_ref[...] = wo_ref[...].astype(BF16)

    nope = lambda h: slice(2 * h * LANES, (2 * h + 1) * LANES)
    rope = lambda h: slice((2 * h + 1) * LANES, (2 * h + 2) * LANES)
    x = x_ref[...]
    if apply_ln:
        x = _layer_norm(x, lng_ref[...], lnb_ref[...])
        h_ref[...] = x
    xb = x.astype(BF16)

    for c0 in range(0, CONV_WIDTH, PREP_CG):
        c1 = c0 + PREP_CG
        ua = _dot(xb, wu_ref[:, c0:c1]) + bglu_ref[:, c0:c1]
        ug = _dot(xb, wu_ref[:, CONV_WIDTH + c0:CONV_WIDTH + c1]) + bglu_ref[:, CONV_WIDTH + c0:CONV_WIDTH + c1]
        win_ref[CONV_HALO:CONV_HALO + tm, c0:c1] = ua * jax.nn.sigmoid(ug)
        for r0 in range(0, tm, CONV_TR):
            pre_ref[r0:r0 + CONV_TR, c0:c1] = _conv_rows(win_ref, r0, CONV_TR, c0, c1, wdw_ref, bdw_ref)
    for r0 in range(0, tm, CONV_TR):
        y = _layer_norm(pre_ref[r0:r0 + CONV_TR, :], gcln_ref[...], bcln_ref[...])
        conv_ref[r0:r0 + CONV_TR, :] = (y * jax.nn.sigmoid(y)).astype(BF16)

    if apply_ln:
        ang = pos_ref[...].astype(F32) * invf_ref[...]
        cos_t = jnp.cos(ang)
        sin_t = jnp.sin(ang)
        cos_ref[...] = cos_t
        sin_ref[...] = sin_t
    else:
        cos_t = cos_ref[...]
        sin_t = sin_ref[...]

    za = _dot(xb, wa_ref[...])
    cq = za[:, :Q_LORA]
    ckv = za[:, Q_LORA:Q_LORA + KV_LORA]
    krb = za[:, Q_LORA + KV_LORA:]

    lane = lax.broadcasted_iota(jnp.int32, krb.shape, 1)
    t = krb * jnp.where(lane < QK_ROPE, cos_t, sin_t)
    kro = t + pltpu.roll(t, QK_ROPE, 1)
    kr = jnp.where(lane < QK_ROPE, kro, 0.0).astype(BF16)

    q = _dot(_rms_norm(cq, gq_ref[...]).astype(BF16), wq_ref[...])
    kv = _dot(_rms_norm(ckv, gkv_ref[...]).astype(BF16), wkv_ref[...])
    for h in range(MLA_HEADS):
        lo = MLA_WIDTH + h * LANES
        sw = 2 * MLA_WIDTH + h * LANES
        q_ref[:, nope(h)] = q[:, h * LANES:(h + 1) * LANES].astype(BF16)
        q_ref[:, rope(h)] = (q[:, lo:lo + LANES] * cos_t + q[:, sw:sw + LANES] * sin_t).astype(BF16)
        k_ref[:, nope(h)] = kv[:, h * LANES:(h + 1) * LANES].astype(BF16)
        k_ref[:, rope(h)] = kr
    v_ref[...] = kv[:, MLA_WIDTH:].astype(BF16)

    win_ref[0:CONV_HALO, :] = win_ref[tm:tm + CONV_HALO, :]


def _prep(x, rope_in, ln, wa, wu, bglu, gq, wq, gkv, wkv, wdw, bdw, gcln, bcln, w2, w_out, layer,
          seq):
    m = x.shape[0]
    tm = PREP_TM
    steps = m // tm
    f2, fo = D_FF // steps, D_MODEL // steps
    apply_ln = ln is not None
    row = lambda w: pl.BlockSpec((tm, w), lambda i: (i, 0))
    if apply_ln:
        in_specs = [row(D_MODEL), row(1), _const_spec((1, LANES))] + [_const_spec((1, D_MODEL))] * 2
        args = [x, *rope_in, *ln]
    else:
        in_specs = [row(D_MODEL), row(LANES), row(LANES)]
        args = [x, *rope_in]
    slab = lambda a: pl.BlockSpec((None,) + a.shape[1:], lambda i: (layer, 0, 0),
                                  pipeline_mode=pl.Buffered(1))
    once = lambda a: pl.BlockSpec(a.shape, lambda i: (0, 0), pipeline_mode=pl.Buffered(1))
    consts = [bglu, gq, wq, gkv, wkv, wdw, bdw, gcln, bcln]
    in_specs += [slab(wa), slab(wu)]
    in_specs += [once(a) if a is wq or a is wkv else _const_spec(a.shape) for a in consts]
    in_specs += [pl.BlockSpec((None, f2, D_MODEL), lambda i: (layer, i, 0)),
                 pl.BlockSpec((None, fo, D_MODEL), lambda i: (layer, i, 0))]
    args += [wa, wu] + consts + [w2, w_out]
    out_shape, out_specs = [], []
    if apply_ln:
        for width in (D_MODEL, LANES, LANES):
            out_shape.append(jax.ShapeDtypeStruct((m, width), F32))
            out_specs.append(row(width))
    for width in (2 * MLA_WIDTH, 2 * MLA_WIDTH, MLA_WIDTH, CONV_WIDTH):
        out_shape.append(jax.ShapeDtypeStruct((m, width), BF16))
        out_specs.append(row(width))
    out_shape += [jax.ShapeDtypeStruct((D_FF, D_MODEL), BF16),
                  jax.ShapeDtypeStruct((D_MODEL, D_MODEL), BF16)]
    out_specs += [pl.BlockSpec((f2, D_MODEL), lambda i: (i, 0)),
                  pl.BlockSpec((fo, D_MODEL), lambda i: (i, 0))]
    return pl.pallas_call(
        functools.partial(_prep_kernel, apply_ln, seq // tm),
        grid=(steps,),
        in_specs=in_specs,
        out_specs=out_specs,
        out_shape=out_shape,
        scratch_shapes=[pltpu.VMEM((tm + CONV_HALO, CONV_WIDTH), F32),
                        pltpu.VMEM((tm, CONV_WIDTH), F32)],
        compiler_params=pltpu.CompilerParams(
            dimension_semantics=("arbitrary",), vmem_limit_bytes=40 * 1024 * 1024),
        name="prep_ln" if apply_ln else "prep",
    )(*args)


def _attn_kernel(q_ref, k_ref, v_ref, w1_ref, o_ref, w1b_ref, s_ref, mx_ref, l_ref, acc_ref):
    w1b_ref[...] = w1_ref[...].astype(BF16)

    i = pl.program_id(1)
    tq = ATTN_TQ
    c = (QK_NOPE + QK_ROPE) ** -0.5 * 1.4426950408889634
    hq = lambda h: slice(2 * h * LANES, 2 * (h + 1) * LANES)
    hv = lambda h: slice(h * LANES, (h + 1) * LANES)

    mx_ref[...] = jnp.full(mx_ref.shape, -jnp.inf, F32)

    def scores(j, masked):
        r0 = pl.multiple_of(j * tq, tq)
        for h in range(MLA_HEADS):
            s = lax.dot_general(q_ref[:, hq(h)], k_ref[pl.ds(r0, tq), hq(h)],
                                (((1,), (1,)), ((), ())), preferred_element_type=F32) * c
            if masked:
                col = lax.broadcasted_iota(jnp.int32, (CHUNK, tq), 1)
                s = jnp.concatenate(
                    [jnp.where(col < (r + 1) * CHUNK, s[r * CHUNK:(r + 1) * CHUNK, :], -1e30)
                     for r in range(tq // CHUNK)], axis=0)
            s_ref[h, j] = s
            mx_ref[h] = jnp.maximum(mx_ref[h], jnp.maximum(s[:, :LANES], s[:, LANES:]))

    def weighted(j):
        r0 = pl.multiple_of(j * tq, tq)
        for h in range(MLA_HEADS):
            m = mx_ref[h]
            p = jnp.exp2(s_ref[h, j] - jnp.concatenate([m, m], axis=1))
            l_ref[h] += p[:, :LANES] + p[:, LANES:]
            acc_ref[h] += _dot(p.astype(BF16), v_ref[pl.ds(r0, tq), hv(h)])

    def pairs(fn, n):
        def two(t, _):
            fn(2 * t)
            fn(2 * t + 1)
            return 0
        lax.fori_loop(0, n // 2, two, 0)

        @pl.when(n % 2 == 1)
        def _():
            fn(n - 1)

    pairs(lambda j: scores(j, False), i)
    scores(i, True)

    for h in range(MLA_HEADS):
        mx_ref[h] = jnp.broadcast_to(jnp.max(mx_ref[h], axis=-1, keepdims=True), (tq, LANES))
    l_ref[...] = jnp.zeros(l_ref.shape, F32)
    acc_ref[...] = jnp.zeros(acc_ref.shape, F32)

    pairs(weighted, i + 1)

    for h in range(MLA_HEADS):
        l = jnp.sum(l_ref[h], axis=-1, keepdims=True)
        o_ref[:, hv(h)] = (acc_ref[h] / l).astype(o_ref.dtype)


def _attention(q, k, v, w1, layer, batch, seq):
    m = q.shape[0]
    tq = ATTN_TQ
    nq = seq // tq
    f1 = D_FF // (batch * nq)
    per_tile = MLP_TF // f1
    step = lambda b, i: b * nq + i
    head_acc = pltpu.VMEM((MLA_HEADS, tq, LANES), F32)
    return pl.pallas_call(
        _attn_kernel,
        grid=(batch, nq),
        in_specs=[pl.BlockSpec((tq, 2 * MLA_WIDTH), lambda b, i: (step(b, i), 0)),
                  pl.BlockSpec((seq, 2 * MLA_WIDTH), lambda b, i: (b, 0)),
                  pl.BlockSpec((seq, MLA_WIDTH), lambda b, i: (b, 0)),
                  pl.BlockSpec((None, D_MODEL, f1), lambda b, i: (layer, 0, step(b, i)))],
        out_specs=[pl.BlockSpec((tq, MLA_WIDTH), lambda b, i: (step(b, i), 0)),
                   pl.BlockSpec((None, D_MODEL, f1),
                                lambda b, i: (step(b, i) // per_tile, 0, step(b, i) % per_tile))],
        out_shape=[jax.ShapeDtypeStruct((m, MLA_WIDTH), BF16),
                   jax.ShapeDtypeStruct((D_FF // MLP_TF, D_MODEL, MLP_TF), BF16)],
        scratch_shapes=[pltpu.VMEM((MLA_HEADS, nq, tq, tq), F32), head_acc, head_acc, head_acc],
        compiler_params=pltpu.CompilerParams(
            dimension_semantics=("arbitrary", "arbitrary"), vmem_limit_bytes=58 * 1024 * 1024),
        name="attention",
    )(q, k, v, w1)


def _outproj_kernel(alpha, a_ref, c_ref, x_ref, w_ref, g_ref, b_ref, o_ref):
    for r in range(0, OUT_TM, OUT_TR):
        rows = slice(r, r + OUT_TR)
        mix = (_dot(a_ref[rows, :], w_ref[0:MLA_WIDTH, :])
               + _dot(c_ref[rows, :], w_ref[MLA_WIDTH:D_MODEL, :]))
        o_ref[rows, :] = _layer_norm(alpha * x_ref[rows, :] + mix, g_ref[...], b_ref[...])


def _outproj(attn, conv, x, w, g, b, alpha):
    m = x.shape[0]
    tm = OUT_TM
    row = lambda wd: pl.BlockSpec((tm, wd), lambda i: (i, 0))
    return pl.pallas_call(
        functools.partial(_outproj_kernel, alpha),
        grid=(m // tm,),
        in_specs=[row(MLA_WIDTH), row(CONV_WIDTH), row(D_MODEL),
                  pl.BlockSpec(w.shape, lambda i: (0, 0), pipeline_mode=pl.Buffered(1)),
                  _const_spec(g.shape), _const_spec(b.shape)],
        out_specs=row(D_MODEL),
        out_shape=jax.ShapeDtypeStruct((m, D_MODEL), F32),
        compiler_params=pltpu.CompilerParams(
            dimension_semantics=("arbitrary",), vmem_limit_bytes=56 * 1024 * 1024),
        name="outproj",
    )(attn, conv, x, w, g, b)


def _mlp_kernel(alpha, x_ref, w1_ref, w2_ref, g_ref, b_ref, o_ref, xb_ref):
    f = pl.program_id(1)
    last = pl.num_programs(1) - 1

    def hidden(xb):
        h = jnp.maximum(_dot(xb, w1_ref[...]), 0.0)
        return (h * h).astype(BF16)

    @pl.when(f == 0)
    def _():
        for r in range(0, MLP_TM, MLP_TR):
            rows = slice(r, r + MLP_TR)
            x = x_ref[rows, :]
            xb = x.astype(BF16)
            xb_ref[rows, :] = xb
            hb = hidden(xb)
            for n in range(0, D_MODEL, MLP_TN):
                o_ref[rows, n:n + MLP_TN] = alpha * x[:, n:n + MLP_TN] + _dot(hb, w2_ref[:, n:n + MLP_TN])

    @pl.when(jnp.logical_and(f > 0, f < last))
    def _():
        hb = hidden(xb_ref[...])
        for n in range(0, D_MODEL, MLP_TN):
            o_ref[:, n:n + MLP_TN] += _dot(hb, w2_ref[:, n:n + MLP_TN])

    @pl.when(f == last)
    def _():
        for r in range(0, MLP_TM, MLP_TR):
            rows = slice(r, r + MLP_TR)
            hb = hidden(xb_ref[rows, :])
            y = o_ref[rows, :] + _dot(hb, w2_ref[...])
            o_ref[rows, :] = _layer_norm(y, g_ref[...], b_ref[...])


def _mlp(x, w1, w2, g, b, alpha):
    m = x.shape[0]
    tm, tf = MLP_TM, MLP_TF
    xspec = pl.BlockSpec((tm, D_MODEL), lambda i, f: (i, 0))
    return pl.pallas_call(
        functools.partial(_mlp_kernel, alpha),
        grid=(m // tm, D_FF // tf),
        in_specs=[xspec, pl.BlockSpec((None, D_MODEL, tf), lambda i, f: (f, 0, 0)),
                  pl.BlockSpec((tf, D_MODEL), lambda i, f: (f, 0)),
                  _const_spec(g.shape), _const_spec(b.shape)],
        out_specs=xspec,
        out_shape=jax.ShapeDtypeStruct((m, D_MODEL), F32),
        scratch_shapes=[pltpu.VMEM((tm, D_MODEL), BF16)],
        compiler_params=pltpu.CompilerParams(
            dimension_semantics=("arbitrary", "arbitrary"), vmem_limit_bytes=58 * 1024 * 1024),
        name="mlp",
    )(x, w1, w2, g, b)


def _swap_halves(w):
    return jnp.concatenate([-w[..., HALF_ROPE:], w[..., :HALF_ROPE]], axis=-1)


def _w_in_kernel(wt_ref, wa_ref, wu_ref):
    kv_end = Q_LORA + KV_LORA
    kr_end = kv_end + QK_ROPE
    wa_ref[:, :kv_end] = wt_ref[:kv_end, :].T.astype(BF16)
    k1 = wt_ref[kv_end:kv_end + HALF_ROPE, :]
    k2 = wt_ref[kv_end + HALF_ROPE:kr_end, :]
    wa_ref[:, kv_end:] = jnp.concatenate([k1, k2, -k2, k1], axis=0).T.astype(BF16)
    wu_ref[...] = wt_ref[kr_end:, :].T.astype(BF16)


def _w_in_layout(w_in):
    depth, d, in_cols = w_in.shape
    w_in_t = jnp.swapaxes(w_in, 1, 2)
    tc = W_IN_TC
    return pl.pallas_call(
        _w_in_kernel,
        grid=(depth, d // tc),
        in_specs=[pl.BlockSpec((None, in_cols, tc), lambda l, i: (l, 0, i))],
        out_specs=[pl.BlockSpec((None, tc, A_COLS), lambda l, i: (l, i, 0)),
                   pl.BlockSpec((None, tc, 2 * CONV_WIDTH), lambda l, i: (l, i, 0))],
        out_shape=[jax.ShapeDtypeStruct((depth, d, A_COLS), BF16),
                   jax.ShapeDtypeStruct((depth, d, 2 * CONV_WIDTH), BF16)],
        compiler_params=pltpu.CompilerParams(dimension_semantics=("arbitrary", "arbitrary")),
        name="w_in_layout",
    )(w_in_t)


def _up_weights(w_uq, w_ukv):
    wq3 = w_uq.reshape(Q_LORA, MLA_HEADS, QK_NOPE + QK_ROPE)
    rope = wq3[:, :, QK_NOPE:]
    pad = jnp.zeros((Q_LORA, MLA_HEADS, LANES - QK_ROPE), w_uq.dtype)
    wq = jnp.concatenate([
        wq3[:, :, :QK_NOPE].reshape(Q_LORA, MLA_WIDTH),
        jnp.concatenate([rope, pad], axis=-1).reshape(Q_LORA, MLA_HEADS * LANES),
        jnp.concatenate([_swap_halves(rope), pad], axis=-1).reshape(Q_LORA, MLA_HEADS * LANES),
    ], axis=1).astype(BF16)

    wkv3 = w_ukv.reshape(KV_LORA, MLA_HEADS, QK_NOPE + V_HEAD)
    wkv = jnp.concatenate([wkv3[:, :, :QK_NOPE].reshape(KV_LORA, MLA_WIDTH),
                           wkv3[:, :, QK_NOPE:].reshape(KV_LORA, MLA_WIDTH)], axis=1).astype(BF16)
    return wq, wkv


def kernel(x, positions, ln_in_g, ln_in_b, w_in, g_q, w_uq, g_kv, w_ukv, b_glu, w_dw, b_dw, g_cln,
           b_cln, w_out, ln1_g, ln1_b, w1, w2, ln2_g, ln2_b):
    batch, seq, d = x.shape
    depth = w_in.shape[0]
    m = batch * seq
    alpha = (2.0 * depth) ** 0.25

    inv_freq = ROPE_THETA ** (-jnp.arange(0, QK_ROPE, 2, dtype=F32) / QK_ROPE)
    invf = jnp.tile(inv_freq, LANES // HALF_ROPE).reshape(1, LANES)
    pos = positions.reshape(m, 1)
    row = lambda a: a.reshape(1, -1)

    wa, wu = _w_in_layout(w_in)
    h = x.reshape(m, d)
    rope_in = (pos, invf)
    for l in range(depth):
        wq, wkv = _up_weights(w_uq[l], w_ukv[l])
        ln = (row(ln_in_g), row(ln_in_b)) if l == 0 else None
        outs = _prep(h, rope_in, ln, wa, wu, row(b_glu[l]), row(g_q[l]), wq, row(g_kv[l]), wkv,
                     w_dw[l], row(b_dw[l]), row(g_cln[l]), row(b_cln[l]), w2, w_out, l, seq)
        if l == 0:
            h, cos_t, sin_t = outs[:3]
            rope_in, outs = (cos_t, sin_t), outs[3:]
        q, k, v, conv, w2_b, wo_b = outs
        attn, w1_b = _attention(q, k, v, w1, l, batch, seq)
        h = _outproj(attn, conv, h, wo_b, row(ln1_g[l]), row(ln1_b[l]), alpha)
        h = _mlp(h, w1_b, w2_b, row(ln2_g[l]), row(ln2_b[l]), alpha)
    return h.reshape(batch, seq, d)
```

```python
import functools

import jax
import jax.numpy as jnp
from jax import lax
from jax.experimental import pallas as pl
from jax.experimental.pallas import tpu as pltpu

D_MODEL = 2048
CHUNK = 64
MLA_HEADS = 8
QK_NOPE = 128
QK_ROPE = 64
V_HEAD = 128
Q_LORA = 512
KV_LORA = 256
MLA_WIDTH = MLA_HEADS * V_HEAD
CONV_WIDTH = D_MODEL - MLA_WIDTH
CONV_K = 31
D_FF = 4 * D_MODEL
ROPE_THETA = 10000.0
LN_EPS = 1e-5
RMS_EPS = 1e-6

LANES = 128
SUBLANES = 8
HALF_ROPE = QK_ROPE // 2
W_IN_TC = 256
A_COLS = Q_LORA + KV_LORA + 2 * QK_ROPE
CONV_HALO = 32

PREP_TM = 256
PREP_CG = 256
ATTN_TQ = 256
CONV_TR = 32
OUT_TM = 512
OUT_TR = 128
MLP_TM = 1024
MLP_TF = 512
MLP_TN = 512

BF16 = jnp.bfloat16
F32 = jnp.float32


def _layer_norm(x, g, b):
    mu = jnp.mean(x, axis=-1, keepdims=True)
    xc = x - mu
    var = jnp.mean(xc * xc, axis=-1, keepdims=True)
    return xc * lax.rsqrt(var + LN_EPS) * g + b


def _rms_norm(x, g):
    return x * lax.rsqrt(jnp.mean(x * x, axis=-1, keepdims=True) + RMS_EPS) * g


def _dot(a, b):
    return jnp.dot(a, b, preferred_element_type=F32)


def _const_spec(shape):
    return pl.BlockSpec(shape, lambda *_: (0,) * len(shape))


def _conv_rows(win_ref, shift_ref, r0, tr, c0, c1, w_ref, b_ref):
    lead = CONV_HALO - (CONV_K - 1)
    blocks = []
    for ci, c in enumerate(range(c0, c1, LANES)):
        slab = win_ref[r0:r0 + tr + CONV_HALO, c:c + LANES]
        y = jnp.broadcast_to(b_ref[:, c:c + LANES], (tr, LANES))
        for b in range(SUBLANES):
            z = None
            nz = tr + (SUBLANES if b else 0)
            for a in range((lead + CONV_K - 1) // SUBLANES + 1):
                k = SUBLANES * a + b - lead
                if 0 <= k < CONV_K:
                    term = slab[SUBLANES * a:SUBLANES * a + nz, :] * w_ref[k:k + 1, c:c + LANES]
                    z = term if z is None else z + term
            if b:
                slot = (r0 // tr) * 2 + ci % 2
                shift_ref[slot, b, :, :] = z[b:b + tr, :]
                z = shift_ref[slot, b, :, :]
            y = y + z
        blocks.append(y)
    return jnp.concatenate(blocks, axis=1)


def _prep_kernel(apply_ln, tiles_per_seq, *refs):
    if apply_ln:
        (x_ref, pos_ref, invf_ref, lng_ref, lnb_ref, wa_ref, wu_ref, bglu_ref, gq_ref, wq_ref,
         gkv_ref, wkv_ref, wdw_ref, bdw_ref, gcln_ref, bcln_ref, w2_ref, wo_ref,
         h_ref, cos_ref, sin_ref, q_ref, k_ref, v_ref, conv_ref, w2b_ref, wob_ref,
         win_ref, pre_ref, shift_ref) = refs
    else:
        (x_ref, cos_ref, sin_ref, wa_ref, wu_ref, bglu_ref, gq_ref, wq_ref,
         gkv_ref, wkv_ref, wdw_ref, bdw_ref, gcln_ref, bcln_ref, w2_ref, wo_ref,
         q_ref, k_ref, v_ref, conv_ref, w2b_ref, wob_ref, win_ref, pre_ref, shift_ref) = refs
    tm = PREP_TM

    @pl.when(pl.program_id(0) % tiles_per_seq == 0)
    def _():
        win_ref[0:CONV_HALO, :] = jnp.zeros((CONV_HALO, CONV_WIDTH), F32)

    w2b_ref[...] = w2_ref[...].astype(BF16)
    wob_ref[...] = wo_ref[...].astype(BF16)

    nope = lambda h: slice(2 * h * LANES, (2 * h + 1) * LANES)
    rope = lambda h: slice((2 * h + 1) * LANES, (2 * h + 2) * LANES)
    x = x_ref[...]
    if apply_ln:
        x = _layer_norm(x, lng_ref[...], lnb_ref[...])
        h_ref[...] = x
    xb = x.astype(BF16)

    for c0 in range(0, CONV_WIDTH, PREP_CG):
        c1 = c0 + PREP_CG
        ua = _dot(xb, wu_ref[:, c0:c1]) + bglu_ref[:, c0:c1]
        ug = _dot(xb, wu_ref[:, CONV_WIDTH + c0:CONV_WIDTH + c1]) + bglu_ref[:, CONV_WIDTH + c0:CONV_WIDTH + c1]
        win_ref[CONV_HALO:CONV_HALO + tm, c0:c1] = ua * jax.nn.sigmoid(ug)
        for r0 in range(0, tm, CONV_TR):
            pre_ref[r0:r0 + CONV_TR, c0:c1] = _conv_rows(win_ref, shift_ref, r0, CONV_TR, c0, c1,
                                                         wdw_ref, bdw_ref)
    for r0 in range(0, tm, CONV_TR):
        y = _layer_norm(pre_ref[r0:r0 + CONV_TR, :], gcln_ref[...], bcln_ref[...])
        conv_ref[r0:r0 + CONV_TR, :] = (y * jax.nn.sigmoid(y)).astype(BF16)

    if apply_ln:
        ang = pos_ref[...].astype(F32) * invf_ref[...]
        cos_t = jnp.cos(ang)
        sin_t = jnp.sin(ang)
        cos_ref[...] = cos_t
        sin_ref[...] = sin_t
    else:
        cos_t = cos_ref[...]
        sin_t = sin_ref[...]

    za = _dot(xb, wa_ref[...])
    cq = za[:, :Q_LORA]
    ckv = za[:, Q_LORA:Q_LORA + KV_LORA]
    krb = za[:, Q_LORA + KV_LORA:]

    lane = lax.broadcasted_iota(jnp.int32, krb.shape, 1)
    t = krb * jnp.where(lane < QK_ROPE, cos_t, sin_t)
    kro = t + pltpu.roll(t, QK_ROPE, 1)
    kr = jnp.where(lane < QK_ROPE, kro, 0.0).astype(BF16)

    q = _dot(_rms_norm(cq, gq_ref[...]).astype(BF16), wq_ref[...])
    kv = _dot(_rms_norm(ckv, gkv_ref[...]).astype(BF16), wkv_ref[...])
    for h in range(MLA_HEADS):
        lo = MLA_WIDTH + h * LANES
        sw = 2 * MLA_WIDTH + h * LANES
        q_ref[:, nope(h)] = q[:, h * LANES:(h + 1) * LANES].astype(BF16)
        q_ref[:, rope(h)] = (q[:, lo:lo + LANES] * cos_t + q[:, sw:sw + LANES] * sin_t).astype(BF16)
        k_ref[:, nope(h)] = kv[:, h * LANES:(h + 1) * LANES].astype(BF16)
        k_ref[:, rope(h)] = kr
    v_ref[...] = kv[:, MLA_WIDTH:].astype(BF16)

    win_ref[0:CONV_HALO, :] = win_ref[tm:tm + CONV_HALO, :]


def _prep(x, rope_in, ln, wa, wu, bglu, gq, wq, gkv, wkv, wdw, bdw, gcln, bcln, w2, w_out, layer,
          seq):
    m = x.shape[0]
    tm = PREP_TM
    steps = m // tm
    f2, fo = D_FF // steps, D_MODEL // steps
    apply_ln = ln is not None
    row = lambda w: pl.BlockSpec((tm, w), lambda i: (i, 0))
    if apply_ln:
        in_specs = [row(D_MODEL), row(1), _const_spec((1, LANES))] + [_const_spec((1, D_MODEL))] * 2
        args = [x, *rope_in, *ln]
    else:
        in_specs = [row(D_MODEL), row(LANES), row(LANES)]
        args = [x, *rope_in]
    slab = lambda a: pl.BlockSpec((None,) + a.shape[1:], lambda i: (layer, 0, 0),
                                  pipeline_mode=pl.Buffered(1))
    once = lambda a: pl.BlockSpec(a.shape, lambda i: (0, 0), pipeline_mode=pl.Buffered(1))
    consts = [bglu, gq, wq, gkv, wkv, wdw, bdw, gcln, bcln]
    in_specs += [slab(wa), slab(wu)]
    in_specs += [once(a) if a is wq or a is wkv else _const_spec(a.shape) for a in consts]
    in_specs += [pl.BlockSpec((None, f2, D_MODEL), lambda i: (layer, i, 0)),
                 pl.BlockSpec((None, fo, D_MODEL), lambda i: (layer, i, 0))]
    args += [wa, wu] + consts + [w2, w_out]
    out_shape, out_specs = [], []
    if apply_ln:
        for width in (D_MODEL, LANES, LANES):
            out_shape.append(jax.ShapeDtypeStruct((m, width), F32))
            out_specs.append(row(width))
    for width in (2 * MLA_WIDTH, 2 * MLA_WIDTH, MLA_WIDTH, CONV_WIDTH):
        out_shape.append(jax.ShapeDtypeStruct((m, width), BF16))
        out_specs.append(row(width))
    out_shape += [jax.ShapeDtypeStruct((D_FF, D_MODEL), BF16),
                  jax.ShapeDtypeStruct((D_MODEL, D_MODEL), BF16)]
    out_specs += [pl.BlockSpec((f2, D_MODEL), lambda i: (i, 0)),
                  pl.BlockSpec((fo, D_MODEL), lambda i: (i, 0))]
    return pl.pallas_call(
        functools.partial(_prep_kernel, apply_ln, seq // tm),
        grid=(steps,),
        in_specs=in_specs,
        out_specs=out_specs,
        out_shape=out_shape,
        scratch_shapes=[pltpu.VMEM((tm + CONV_HALO, CONV_WIDTH), F32),
                        pltpu.VMEM((tm, CONV_WIDTH), F32),
                        pltpu.VMEM((2 * tm // CONV_TR, SUBLANES, CONV_TR, LANES), F32)],
        compiler_params=pltpu.CompilerParams(
            dimension_semantics=("arbitrary",), vmem_limit_bytes=52 * 1024 * 1024),
        name="prep_ln" if apply_ln else "prep",
    )(*args)


def _attn_kernel(q_ref, k_ref, v_ref, w1_ref, o_ref, w1b_ref, s_ref, mx_ref, l_ref, acc_ref):
    w1b_ref[...] = w1_ref[...].astype(BF16)

    i = pl.program_id(1)
    tq = ATTN_TQ
    c = (QK_NOPE + QK_ROPE) ** -0.5 * 1.4426950408889634
    hq = lambda h: slice(2 * h * LANES, 2 * (h + 1) * LANES)
    hv = lambda h: slice(h * LANES, (h + 1) * LANES)

    mx_ref[...] = jnp.full(mx_ref.shape, -jnp.inf, F32)

    def scores(j, masked):
        r0 = pl.multiple_of(j * tq, tq)
        for h in range(MLA_HEADS):
            s = lax.dot_general(q_ref[:, hq(h)], k_ref[pl.ds(r0, tq), hq(h)],
                                (((1,), (1,)), ((), ())), preferred_element_type=F32) * c
            if masked:
                col = lax.broadcasted_iota(jnp.int32, (CHUNK, tq), 1)
                s = jnp.concatenate(
                    [jnp.where(col < (r + 1) * CHUNK, s[r * CHUNK:(r + 1) * CHUNK, :], -1e30)
                     for r in range(tq // CHUNK)], axis=0)
            s_ref[h, j] = s
            mx_ref[h] = jnp.maximum(mx_ref[h], jnp.maximum(s[:, :LANES], s[:, LANES:]))

    def weighted(j):
        r0 = pl.multiple_of(j * tq, tq)
        for h in range(MLA_HEADS):
            m = mx_ref[h]
            p = jnp.exp2(s_ref[h, j] - jnp.concatenate([m, m], axis=1))
            l_ref[h] += p[:, :LANES] + p[:, LANES:]
            acc_ref[h] += _dot(p.astype(BF16), v_ref[pl.ds(r0, tq), hv(h)])

    def pairs(fn, n):
        def two(t, _):
            fn(2 * t)
            fn(2 * t + 1)
            return 0
        lax.fori_loop(0, n // 2, two, 0)

        @pl.when(n % 2 == 1)
        def _():
            fn(n - 1)

    pairs(lambda j: scores(j, False), i)
    scores(i, True)

    for h in range(MLA_HEADS):
        mx_ref[h] = jnp.broadcast_to(jnp.max(mx_ref[h], axis=-1, keepdims=True), (tq, LANES))
    l_ref[...] = jnp.zeros(l_ref.shape, F32)
    acc_ref[...] = jnp.zeros(acc_ref.shape, F32)

    pairs(weighted, i + 1)

    for h in range(MLA_HEADS):
        l = jnp.sum(l_ref[h], axis=-1, keepdims=True)
        o_ref[:, hv(h)] = (acc_ref[h] / l).astype(o_ref.dtype)


def _attention(q, k, v, w1, layer, batch, seq):
    m = q.shape[0]
    tq = ATTN_TQ
    nq = seq // tq
    f1 = D_FF // (batch * nq)
    step = lambda b, i: b * nq + i
    head_acc = pltpu.VMEM((MLA_HEADS, tq, LANES), F32)
    return pl.pallas_call(
        _attn_kernel,
        grid=(batch, nq),
        in_specs=[pl.BlockSpec((tq, 2 * MLA_WIDTH), lambda b, i: (step(b, i), 0)),
                  pl.BlockSpec((seq, 2 * MLA_WIDTH), lambda b, i: (b, 0)),
                  pl.BlockSpec((seq, MLA_WIDTH), lambda b, i: (b, 0)),
                  pl.BlockSpec((None, D_MODEL, f1), lambda b, i: (layer, 0, step(b, i)))],
        out_specs=[pl.BlockSpec((tq, MLA_WIDTH), lambda b, i: (step(b, i), 0)),
                   pl.BlockSpec((D_MODEL, f1), lambda b, i: (0, step(b, i)))],
        out_shape=[jax.ShapeDtypeStruct((m, MLA_WIDTH), BF16),
                   jax.ShapeDtypeStruct((D_MODEL, D_FF), BF16)],
        scratch_shapes=[pltpu.VMEM((MLA_HEADS, nq, tq, tq), F32), head_acc, head_acc, head_acc],
        compiler_params=pltpu.CompilerParams(
            dimension_semantics=("arbitrary", "arbitrary"), vmem_limit_bytes=58 * 1024 * 1024),
        name="attention",
    )(q, k, v, w1)


def _outproj_kernel(alpha, a_ref, c_ref, x_ref, w_ref, g_ref, b_ref, o_ref):
    for r in range(0, OUT_TM, OUT_TR):
        rows = slice(r, r + OUT_TR)
        mix = (_dot(a_ref[rows, :], w_ref[0:MLA_WIDTH, :])
               + _dot(c_ref[rows, :], w_ref[MLA_WIDTH:D_MODEL, :]))
        o_ref[rows, :] = _layer_norm(alpha * x_ref[rows, :] + mix, g_ref[...], b_ref[...])


def _outproj(attn, conv, x, w, g, b, alpha):
    m = x.shape[0]
    tm = OUT_TM
    row = lambda wd: pl.BlockSpec((tm, wd), lambda i: (i, 0))
    return pl.pallas_call(
        functools.partial(_outproj_kernel, alpha),
        grid=(m // tm,),
        in_specs=[row(MLA_WIDTH), row(CONV_WIDTH), row(D_MODEL),
                  pl.BlockSpec(w.shape, lambda i: (0, 0), pipeline_mode=pl.Buffered(1)),
                  _const_spec(g.shape), _const_spec(b.shape)],
        out_specs=row(D_MODEL),
        out_shape=jax.ShapeDtypeStruct((m, D_MODEL), F32),
        compiler_params=pltpu.CompilerParams(
            dimension_semantics=("arbitrary",), vmem_limit_bytes=56 * 1024 * 1024),
        name="outproj",
    )(attn, conv, x, w, g, b)


def _mlp_kernel(alpha, x_ref, w1_ref, w2_ref, g_ref, b_ref, o_ref, xb_ref):
    f = pl.program_id(1)

    @pl.when(f == 0)
    def _():
        x = x_ref[...]
        xb_ref[...] = x.astype(BF16)
        o_ref[...] = alpha * x

    h = jnp.maximum(_dot(xb_ref[...], w1_ref[...]), 0.0)
    hb = (h * h).astype(BF16)
    for n in range(0, D_MODEL, MLP_TN):
        o_ref[:, n:n + MLP_TN] += _dot(hb, w2_ref[:, n:n + MLP_TN])

    @pl.when(f == pl.num_programs(1) - 1)
    def _():
        o_ref[...] = _layer_norm(o_ref[...], g_ref[...], b_ref[...])


def _mlp(x, w1, w2, g, b, alpha):
    m = x.shape[0]
    tm, tf = MLP_TM, MLP_TF
    xspec = pl.BlockSpec((tm, D_MODEL), lambda i, f: (i, 0))
    return pl.pallas_call(
        functools.partial(_mlp_kernel, alpha),
        grid=(m // tm, D_FF // tf),
        in_specs=[xspec, pl.BlockSpec((D_MODEL, tf), lambda i, f: (0, f)),
                  pl.BlockSpec((tf, D_MODEL), lambda i, f: (f, 0)),
                  _const_spec(g.shape), _const_spec(b.shape)],
        out_specs=xspec,
        out_shape=jax.ShapeDtypeStruct((m, D_MODEL), F32),
        scratch_shapes=[pltpu.VMEM((tm, D_MODEL), BF16)],
        compiler_params=pltpu.CompilerParams(
            dimension_semantics=("arbitrary", "arbitrary"), vmem_limit_bytes=58 * 1024 * 1024),
        name="mlp",
    )(x, w1, w2, g, b)


def _swap_halves(w):
    return jnp.concatenate([-w[..., HALF_ROPE:], w[..., :HALF_ROPE]], axis=-1)


def _w_in_kernel(wt_ref, wa_ref, wu_ref):
    kv_end = Q_LORA + KV_LORA
    kr_end = kv_end + QK_ROPE
    wa_ref[:, :kv_end] = wt_ref[:kv_end, :].T.astype(BF16)
    k1 = wt_ref[kv_end:kv_end + HALF_ROPE, :]
    k2 = wt_ref[kv_end + HALF_ROPE:kr_end, :]
    wa_ref[:, kv_end:] = jnp.concatenate([k1, k2, -k2, k1], axis=0).T.astype(BF16)
    wu_ref[...] = wt_ref[kr_end:, :].T.astype(BF16)


def _w_in_layout(w_in):
    depth, d, in_cols = w_in.shape
    w_in_t = jnp.swapaxes(w_in, 1, 2)
    tc = W_IN_TC
    return pl.pallas_call(
        _w_in_kernel,
        grid=(depth, d // tc),
        in_specs=[pl.BlockSpec((None, in_cols, tc), lambda l, i: (l, 0, i))],
        out_specs=[pl.BlockSpec((None, tc, A_COLS), lambda l, i: (l, i, 0)),
                   pl.BlockSpec((None, tc, 2 * CONV_WIDTH), lambda l, i: (l, i, 0))],
        out_shape=[jax.ShapeDtypeStruct((depth, d, A_COLS), BF16),
                   jax.ShapeDtypeStruct((depth, d, 2 * CONV_WIDTH), BF16)],
        compiler_params=pltpu.CompilerParams(dimension_semantics=("arbitrary", "arbitrary")),
        name="w_in_layout",
    )(w_in_t)


def _up_weights(w_uq, w_ukv):
    wq3 = w_uq.reshape(Q_LORA, MLA_HEADS, QK_NOPE + QK_ROPE)
    rope = wq3[:, :, QK_NOPE:]
    pad = jnp.zeros((Q_LORA, MLA_HEADS, LANES - QK_ROPE), w_uq.dtype)
    wq = jnp.concatenate([
        wq3[:, :, :QK_NOPE].reshape(Q_LORA, MLA_WIDTH),
        jnp.concatenate([rope, pad], axis=-1).reshape(Q_LORA, MLA_HEADS * LANES),
        jnp.concatenate([_swap_halves(rope), pad], axis=-1).reshape(Q_LORA, MLA_HEADS * LANES),
    ], axis=1).astype(BF16)

    wkv3 = w_ukv.reshape(KV_LORA, MLA_HEADS, QK_NOPE + V_HEAD)
    wkv = jnp.concatenate([wkv3[:, :, :QK_NOPE].reshape(KV_LORA, MLA_WIDTH),
                           wkv3[:, :, QK_NOPE:].reshape(KV_LORA, MLA_WIDTH)], axis=1).astype(BF16)
    return wq, wkv


def kernel(x, positions, ln_in_g, ln_in_b, w_in, g_q, w_uq, g_kv, w_ukv, b_glu, w_dw, b_dw, g_cln,
           b_cln, w_out, ln1_g, ln1_b, w1, w2, ln2_g, ln2_b):
    batch, seq, d = x.shape
    depth = w_in.shape[0]
    m = batch * seq
    alpha = (2.0 * depth) ** 0.25

    inv_freq = ROPE_THETA ** (-jnp.arange(0, QK_ROPE, 2, dtype=F32) / QK_ROPE)
    invf = jnp.tile(inv_freq, LANES // HALF_ROPE).reshape(1, LANES)
    pos = positions.reshape(m, 1)
    row = lambda a: a.reshape(1, -1)

    wa, wu = _w_in_layout(w_in)
    h = x.reshape(m, d)
    rope_in = (pos, invf)
    for l in range(depth):
        wq, wkv = _up_weights(w_uq[l], w_ukv[l])
        ln = (row(ln_in_g), row(ln_in_b)) if l == 0 else None
        outs = _prep(h, rope_in, ln, wa, wu, row(b_glu[l]), row(g_q[l]), wq, row(g_kv[l]), wkv,
                     w_dw[l], row(b_dw[l]), row(g_cln[l]), row(b_cln[l]), w2, w_out, l, seq)
        if l == 0:
            h, cos_t, sin_t = outs[:3]
            rope_in, outs = (cos_t, sin_t), outs[3:]
        q, k, v, conv, w2_b, wo_b = outs
        attn, w1_b = _attention(q, k, v, w1, l, batch, seq)
        h = _outproj(attn, conv, h, wo_b, row(ln1_g[l]), row(ln1_b[l]), alpha)
        h = _mlp(h, w1_b, w2_b, row(ln2_g[l]), row(ln2_b[l]), alpha)
    return h.reshape(batch, seq, d)
```

```python
import functools

import jax
import jax.numpy as jnp
from jax import lax
from jax.experimental import pallas as pl
from jax.experimental.pallas import tpu as pltpu

D_MODEL = 2048
CHUNK = 64
MLA_HEADS = 8
QK_NOPE = 128
QK_ROPE = 64
V_HEAD = 128
Q_LORA = 512
KV_LORA = 256
MLA_WIDTH = MLA_HEADS * V_HEAD
CONV_WIDTH = D_MODEL - MLA_WIDTH
CONV_K = 31
D_FF = 4 * D_MODEL
ROPE_THETA = 10000.0
LN_EPS = 1e-5
RMS_EPS = 1e-6

LANES = 128
SUBLANES = 8
HALF_ROPE = QK_ROPE // 2
W_IN_TC = 256
A_COLS = Q_LORA + KV_LORA + 2 * QK_ROPE
CONV_HALO = 32

PREP_TM = 256
PREP_CG = 256
ATTN_TQ = 256
CONV_TR = 32
OUT_TM = 512
OUT_TR = 128
MLP_TM = 1024
MLP_TF = 512
MLP_TN = 512

BF16 = jnp.bfloat16
F32 = jnp.float32


def _layer_norm(x, g, b):
    mu = jnp.mean(x, axis=-1, keepdims=True)
    xc = x - mu
    var = jnp.mean(xc * xc, axis=-1, keepdims=True)
    return xc * lax.rsqrt(var + LN_EPS) * g + b


def _rms_norm(x, g):
    return x * lax.rsqrt(jnp.mean(x * x, axis=-1, keepdims=True) + RMS_EPS) * g


def _dot(a, b):
    return jnp.dot(a, b, preferred_element_type=F32)


def _const_spec(shape):
    return pl.BlockSpec(shape, lambda *_: (0,) * len(shape))


def _conv_rows(win_ref, shift_ref, r0, tr, c0, c1, w_ref, b_ref):
    lead = CONV_HALO - (CONV_K - 1)
    blocks = []
    for ci, c in enumerate(range(c0, c1, LANES)):
        slab = win_ref[r0:r0 + tr + CONV_HALO, c:c + LANES]
        y = jnp.broadcast_to(b_ref[:, c:c + LANES], (tr, LANES))
        for b in range(SUBLANES):
            z = None
            nz = tr + (SUBLANES if b else 0)
            for a in range((lead + CONV_K - 1) // SUBLANES + 1):
                k = SUBLANES * a + b - lead
                if 0 <= k < CONV_K:
                    term = slab[SUBLANES * a:SUBLANES * a + nz, :] * w_ref[k:k + 1, c:c + LANES]
                    z = term if z is None else z + term
            if b:
                slot = (r0 // tr) * 2 + ci % 2
                shift_ref[slot, b, :, :] = z[b:b + tr, :]
                z = shift_ref[slot, b, :, :]
            y = y + z
        blocks.append(y)
    return jnp.concatenate(blocks, axis=1)


def _prep_kernel(apply_ln, tiles_per_seq, *refs):
    if apply_ln:
        (x_ref, pos_ref, invf_ref, lng_ref, lnb_ref, wa_ref, wu_ref, bglu_ref, gq_ref, wq_ref,
         gkv_ref, wkv_ref, wdw_ref, bdw_ref, gcln_ref, bcln_ref, w2_ref, wo_ref,
         h_ref, cos_ref, sin_ref, q_ref, k_ref, v_ref, conv_ref, w2b_ref, wob_ref,
         win_ref, pre_ref, shift_ref) = refs
    else:
        (x_ref, cos_ref, sin_ref, wa_ref, wu_ref, bglu_ref, gq_ref, wq_ref,
         gkv_ref, wkv_ref, wdw_ref, bdw_ref, gcln_ref, bcln_ref, w2_ref, wo_ref,
         q_ref, k_ref, v_ref, conv_ref, w2b_ref, wob_ref, win_ref, pre_ref, shift_ref) = refs
    tm = PREP_TM

    @pl.when(pl.program_id(0) % tiles_per_seq == 0)
    def _():
        win_ref[0:CONV_HALO, :] = jnp.zeros((CONV_HALO, CONV_WIDTH), F32)

    w2b_ref[...] = w2_ref[...].astype(BF16)
    wob_ref[...] = wo_ref[...].astype(BF16)

    nope = lambda h: slice(2 * h * LANES, (2 * h + 1) * LANES)
    rope = lambda h: slice((2 * h + 1) * LANES, (2 * h + 2) * LANES)
    x = x_ref[...]
    if apply_ln:
        x = _layer_norm(x, lng_ref[...], lnb_ref[...])
        h_ref[...] = x
    xb = x.astype(BF16)

    for c0 in range(0, CONV_WIDTH, PREP_CG):
        c1 = c0 + PREP_CG
        ua = _dot(xb, wu_ref[:, c0:c1]) + bglu_ref[:, c0:c1]
        ug = _dot(xb, wu_ref[:, CONV_WIDTH + c0:CONV_WIDTH + c1]) + bglu_ref[:, CONV_WIDTH + c0:CONV_WIDTH + c1]
        win_ref[CONV_HALO:CONV_HALO + tm, c0:c1] = ua * jax.nn.sigmoid(ug)
        for r0 in range(0, tm, CONV_TR):
            pre_ref[r0:r0 + CONV_TR, c0:c1] = _conv_rows(win_ref, shift_ref, r0, CONV_TR, c0, c1,
                                                         wdw_ref, bdw_ref)
    for r0 in range(0, tm, CONV_TR):
        y = _layer_norm(pre_ref[r0:r0 + CONV_TR, :], gcln_ref[...], bcln_ref[...])
        conv_ref[r0:r0 + CONV_TR, :] = (y * jax.nn.sigmoid(y)).astype(BF16)

    if apply_ln:
        ang = pos_ref[...].astype(F32) * invf_ref[...]
        cos_t = jnp.cos(ang)
        sin_t = jnp.sin(ang)
        cos_ref[...] = cos_t
        sin_ref[...] = sin_t
    else:
        cos_t = cos_ref[...]
        sin_t = sin_ref[...]

    za = _dot(xb, wa_ref[...])
    cq = za[:, :Q_LORA]
    ckv = za[:, Q_LORA:Q_LORA + KV_LORA]
    krb = za[:, Q_LORA + KV_LORA:]

    lane = lax.broadcasted_iota(jnp.int32, krb.shape, 1)
    t = krb * jnp.where(lane < QK_ROPE, cos_t, sin_t)
    kro = t + pltpu.roll(t, QK_ROPE, 1)
    kr = jnp.where(lane < QK_ROPE, kro, 0.0).astype(BF16)

    q = _dot(_rms_norm(cq, gq_ref[...]).astype(BF16), wq_ref[...])
    kv = _dot(_rms_norm(ckv, gkv_ref[...]).astype(BF16), wkv_ref[...])
    for h in range(MLA_HEADS):
        lo = MLA_WIDTH + h * LANES
        sw = 2 * MLA_WIDTH + h * LANES
        q_ref[:, nope(h)] = q[:, h * LANES:(h + 1) * LANES].astype(BF16)
        q_ref[:, rope(h)] = (q[:, lo:lo + LANES] * cos_t + q[:, sw:sw + LANES] * sin_t).astype(BF16)
        k_ref[:, nope(h)] = kv[:, h * LANES:(h + 1) * LANES].astype(BF16)
        k_ref[:, rope(h)] = kr
    v_ref[...] = kv[:, MLA_WIDTH:].astype(BF16)

    win_ref[0:CONV_HALO, :] = win_ref[tm:tm + CONV_HALO, :]


def _prep(x, rope_in, ln, wa, wu, bglu, gq, wq, gkv, wkv, wdw, bdw, gcln, bcln, w2, w_out, layer,
          seq):
    m = x.shape[0]
    tm = PREP_TM
    steps = m // tm
    f2, fo = D_FF // steps, D_MODEL // steps
    apply_ln = ln is not None
    row = lambda w: pl.BlockSpec((tm, w), lambda i: (i, 0))
    if apply_ln:
        in_specs = [row(D_MODEL), row(1), _const_spec((1, LANES))] + [_const_spec((1, D_MODEL))] * 2
        args = [x, *rope_in, *ln]
    else:
        in_specs = [row(D_MODEL), row(LANES), row(LANES)]
        args = [x, *rope_in]
    slab = lambda a: pl.BlockSpec((None,) + a.shape[1:], lambda i: (layer, 0, 0),
                                  pipeline_mode=pl.Buffered(1))
    once = lambda a: pl.BlockSpec(a.shape, lambda i: (0, 0), pipeline_mode=pl.Buffered(1))
    consts = [bglu, gq, wq, gkv, wkv, wdw, bdw, gcln, bcln]
    in_specs += [slab(wa), slab(wu)]
    in_specs += [once(a) if a is wq or a is wkv else _const_spec(a.shape) for a in consts]
    in_specs += [pl.BlockSpec((None, f2, D_MODEL), lambda i: (layer, i, 0)),
                 pl.BlockSpec((None, fo, D_MODEL), lambda i: (layer, i, 0))]
    args += [wa, wu] + consts + [w2, w_out]
    out_shape, out_specs = [], []
    if apply_ln:
        for width in (D_MODEL, LANES, LANES):
            out_shape.append(jax.ShapeDtypeStruct((m, width), F32))
            out_specs.append(row(width))
    for width in (2 * MLA_WIDTH, 2 * MLA_WIDTH, MLA_WIDTH, CONV_WIDTH):
        out_shape.append(jax.ShapeDtypeStruct((m, width), BF16))
        out_specs.append(row(width))
    out_shape += [jax.ShapeDtypeStruct((D_FF, D_MODEL), BF16),
                  jax.ShapeDtypeStruct((D_MODEL, D_MODEL), BF16)]
    out_specs += [pl.BlockSpec((f2, D_MODEL), lambda i: (i, 0)),
                  pl.BlockSpec((fo, D_MODEL), lambda i: (i, 0))]
    return pl.pallas_call(
        functools.partial(_prep_kernel, apply_ln, seq // tm),
        grid=(steps,),
        in_specs=in_specs,
        out_specs=out_specs,
        out_shape=out_shape,
        scratch_shapes=[pltpu.VMEM((tm + CONV_HALO, CONV_WIDTH), F32),
                        pltpu.VMEM((tm, CONV_WIDTH), F32),
                        pltpu.VMEM((2 * tm // CONV_TR, SUBLANES, CONV_TR, LANES), F32)],
        compiler_params=pltpu.CompilerParams(
            dimension_semantics=("arbitrary",), vmem_limit_bytes=52 * 1024 * 1024),
        name="prep_ln" if apply_ln else "prep",
    )(*args)


def _attn_kernel(q_ref, k_ref, v_ref, w1_ref, o_ref, w1b_ref, s_ref, mx_ref, l_ref, acc_ref):
    w1b_ref[...] = w1_ref[...].astype(BF16)

    i = pl.program_id(1)
    tq = ATTN_TQ
    c = (QK_NOPE + QK_ROPE) ** -0.5 * 1.4426950408889634
    hq = lambda h: slice(2 * h * LANES, 2 * (h + 1) * LANES)
    hv = lambda h: slice(h * LANES, (h + 1) * LANES)

    mx_ref[...] = jnp.full(mx_ref.shape, -jnp.inf, F32)

    def scores(j, masked):
        r0 = pl.multiple_of(j * tq, tq)
        for h in range(MLA_HEADS):
            s = lax.dot_general(q_ref[:, hq(h)], k_ref[pl.ds(r0, tq), hq(h)],
                                (((1,), (1,)), ((), ())), preferred_element_type=F32) * c
            if masked:
                col = lax.broadcasted_iota(jnp.int32, (CHUNK, tq), 1)
                s = jnp.concatenate(
                    [jnp.where(col < (r + 1) * CHUNK, s[r * CHUNK:(r + 1) * CHUNK, :], -1e30)
                     for r in range(tq // CHUNK)], axis=0)
            s_ref[h, j] = s
            mx_ref[h] = jnp.maximum(mx_ref[h], jnp.maximum(s[:, :LANES], s[:, LANES:]))

    def weighted(j):
        r0 = pl.multiple_of(j * tq, tq)
        for h in range(MLA_HEADS):
            m = mx_ref[h]
            p = jnp.exp2(s_ref[h, j] - jnp.concatenate([m, m], axis=1))
            l_ref[h] += p[:, :LANES] + p[:, LANES:]
            acc_ref[h] += _dot(p.astype(BF16), v_ref[pl.ds(r0, tq), hv(h)])

    def pairs(fn, n):
        def four(t, _):
            for u in range(4):
                fn(4 * t + u)
            return 0
        lax.fori_loop(0, n // 4, four, 0)

        @pl.when(n % 4 >= 2)
        def _():
            fn((n // 4) * 4)
            fn((n // 4) * 4 + 1)

        @pl.when(n % 2 == 1)
        def _():
            fn(n - 1)

    pairs(lambda j: scores(j, False), i)
    scores(i, True)

    for h in range(MLA_HEADS):
        mx_ref[h] = jnp.broadcast_to(jnp.max(mx_ref[h], axis=-1, keepdims=True), (tq, LANES))
    l_ref[...] = jnp.zeros(l_ref.shape, F32)
    acc_ref[...] = jnp.zeros(acc_ref.shape, F32)

    pairs(weighted, i + 1)

    for h in range(MLA_HEADS):
        l = jnp.sum(l_ref[h], axis=-1, keepdims=True)
        o_ref[:, hv(h)] = (acc_ref[h] / l).astype(o_ref.dtype)


def _attention(q, k, v, w1, layer, batch, seq):
    m = q.shape[0]
    tq = ATTN_TQ
    nq = seq // tq
    f1 = D_FF // (batch * nq)
    step = lambda b, i: b * nq + i
    head_acc = pltpu.VMEM((MLA_HEADS, tq, LANES), F32)
    return pl.pallas_call(
        _attn_kernel,
        grid=(batch, nq),
        in_specs=[pl.BlockSpec((tq, 2 * MLA_WIDTH), lambda b, i: (step(b, i), 0)),
                  pl.BlockSpec((seq, 2 * MLA_WIDTH), lambda b, i: (b, 0)),
                  pl.BlockSpec((seq, MLA_WIDTH), lambda b, i: (b, 0)),
                  pl.BlockSpec((None, D_MODEL, f1), lambda b, i: (layer, 0, step(b, i)))],
        out_specs=[pl.BlockSpec((tq, MLA_WIDTH), lambda b, i: (step(b, i), 0)),
                   pl.BlockSpec((D_MODEL, f1), lambda b, i: (0, step(b, i)))],
        out_shape=[jax.ShapeDtypeStruct((m, MLA_WIDTH), BF16),
                   jax.ShapeDtypeStruct((D_MODEL, D_FF), BF16)],
        scratch_shapes=[pltpu.VMEM((MLA_HEADS, nq, tq, tq), F32), head_acc, head_acc, head_acc],
        compiler_params=pltpu.CompilerParams(
            dimension_semantics=("arbitrary", "arbitrary"), vmem_limit_bytes=58 * 1024 * 1024),
        name="attention",
    )(q, k, v, w1)


def _outproj_kernel(alpha, a_ref, c_ref, x_ref, w_ref, g_ref, b_ref, o_ref):
    for r in range(0, OUT_TM, OUT_TR):
        rows = slice(r, r + OUT_TR)
        mix = (_dot(a_ref[rows, :], w_ref[0:MLA_WIDTH, :])
               + _dot(c_ref[rows, :], w_ref[MLA_WIDTH:D_MODEL, :]))
        o_ref[rows, :] = _layer_norm(alpha * x_ref[rows, :] + mix, g_ref[...], b_ref[...])


def _outproj(attn, conv, x, w, g, b, alpha):
    m = x.shape[0]
    tm = OUT_TM
    row = lambda wd: pl.BlockSpec((tm, wd), lambda i: (i, 0))
    return pl.pallas_call(
        functools.partial(_outproj_kernel, alpha),
        grid=(m // tm,),
        in_specs=[row(MLA_WIDTH), row(CONV_WIDTH), row(D_MODEL),
                  pl.BlockSpec(w.shape, lambda i: (0, 0), pipeline_mode=pl.Buffered(1)),
                  _const_spec(g.shape), _const_spec(b.shape)],
        out_specs=row(D_MODEL),
        out_shape=jax.ShapeDtypeStruct((m, D_MODEL), F32),
        compiler_params=pltpu.CompilerParams(
            dimension_semantics=("arbitrary",), vmem_limit_bytes=56 * 1024 * 1024),
        name="outproj",
    )(attn, conv, x, w, g, b)


def _mlp_kernel(alpha, x_ref, w1_ref, w2_ref, g_ref, b_ref, o_ref, xb_ref):
    f = pl.program_id(1)

    @pl.when(f == 0)
    def _():
        x = x_ref[...]
        xb_ref[...] = x.astype(BF16)
        o_ref[...] = alpha * x

    h = jnp.maximum(_dot(xb_ref[...], w1_ref[...]), 0.0)
    hb = (h * h).astype(BF16)
    for n in range(0, D_MODEL, MLP_TN):
        o_ref[:, n:n + MLP_TN] += _dot(hb, w2_ref[:, n:n + MLP_TN])

    @pl.when(f == pl.num_programs(1) - 1)
    def _():
        o_ref[...] = _layer_norm(o_ref[...], g_ref[...], b_ref[...])


def _mlp(x, w1, w2, g, b, alpha):
    m = x.shape[0]
    tm, tf = MLP_TM, MLP_TF
    xspec = pl.BlockSpec((tm, D_MODEL), lambda i, f: (i, 0))
    return pl.pallas_call(
        functools.partial(_mlp_kernel, alpha),
        grid=(m // tm, D_FF // tf),
        in_specs=[xspec, pl.BlockSpec((D_MODEL, tf), lambda i, f: (0, f)),
                  pl.BlockSpec((tf, D_MODEL), lambda i, f: (f, 0)),
                  _const_spec(g.shape), _const_spec(b.shape)],
        out_specs=xspec,
        out_shape=jax.ShapeDtypeStruct((m, D_MODEL), F32),
        scratch_shapes=[pltpu.VMEM((tm, D_MODEL), BF16)],
        compiler_params=pltpu.CompilerParams(
            dimension_semantics=("arbitrary", "arbitrary"), vmem_limit_bytes=58 * 1024 * 1024),
        name="mlp",
    )(x, w1, w2, g, b)


def _swap_halves(w):
    return jnp.concatenate([-w[..., HALF_ROPE:], w[..., :HALF_ROPE]], axis=-1)


def _w_in_kernel(wt_ref, wa_ref, wu_ref):
    kv_end = Q_LORA + KV_LORA
    kr_end = kv_end + QK_ROPE
    wa_ref[:, :kv_end] = wt_ref[:kv_end, :].T.astype(BF16)
    k1 = wt_ref[kv_end:kv_end + HALF_ROPE, :]
    k2 = wt_ref[kv_end + HALF_ROPE:kr_end, :]
    wa_ref[:, kv_end:] = jnp.concatenate([k1, k2, -k2, k1], axis=0).T.astype(BF16)
    wu_ref[...] = wt_ref[kr_end:, :].T.astype(BF16)


def _w_in_layout(w_in):
    depth, d, in_cols = w_in.shape
    w_in_t = jnp.swapaxes(w_in, 1, 2)
    tc = W_IN_TC
    return pl.pallas_call(
        _w_in_kernel,
        grid=(depth, d // tc),
        in_specs=[pl.BlockSpec((None, in_cols, tc), lambda l, i: (l, 0, i))],
        out_specs=[pl.BlockSpec((None, tc, A_COLS), lambda l, i: (l, i, 0)),
                   pl.BlockSpec((None, tc, 2 * CONV_WIDTH), lambda l, i: (l, i, 0))],
        out_shape=[jax.ShapeDtypeStruct((depth, d, A_COLS), BF16),
                   jax.ShapeDtypeStruct((depth, d, 2 * CONV_WIDTH), BF16)],
        compiler_params=pltpu.CompilerParams(dimension_semantics=("arbitrary", "arbitrary")),
        name="w_in_layout",
    )(w_in_t)


def _up_weights(w_uq, w_ukv):
    wq3 = w_uq.reshape(Q_LORA, MLA_HEADS, QK_NOPE + QK_ROPE)
    rope = wq3[:, :, QK_NOPE:]
    pad = jnp.zeros((Q_LORA, MLA_HEADS, LANES - QK_ROPE), w_uq.dtype)
    wq = jnp.concatenate([
        wq3[:, :, :QK_NOPE].reshape(Q_LORA, MLA_WIDTH),
        jnp.concatenate([rope, pad], axis=-1).reshape(Q_LORA, MLA_HEADS * LANES),
        jnp.concatenate([_swap_halves(rope), pad], axis=-1).reshape(Q_LORA, MLA_HEADS * LANES),
    ], axis=1).astype(BF16)

    wkv3 = w_ukv.reshape(KV_LORA, MLA_HEADS, QK_NOPE + V_HEAD)
    wkv = jnp.concatenate([wkv3[:, :, :QK_NOPE].reshape(KV_LORA, MLA_WIDTH),
                           wkv3[:, :, QK_NOPE:].reshape(KV_LORA, MLA_WIDTH)], axis=1).astype(BF16)
    return wq, wkv


def kernel(x, positions, ln_in_g, ln_in_b, w_in, g_q, w_uq, g_kv, w_ukv, b_glu, w_dw, b_dw, g_cln,
           b_cln, w_out, ln1_g, ln1_b, w1, w2, ln2_g, ln2_b):
    batch, seq, d = x.shape
    depth = w_in.shape[0]
    m = batch * seq
    alpha = (2.0 * depth) ** 0.25

    inv_freq = ROPE_THETA ** (-jnp.arange(0, QK_ROPE, 2, dtype=F32) / QK_ROPE)
    invf = jnp.tile(inv_freq, LANES // HALF_ROPE).reshape(1, LANES)
    pos = positions.reshape(m, 1)
    row = lambda a: a.reshape(1, -1)

    wa, wu = _w_in_layout(w_in)
    h = x.reshape(m, d)
    rope_in = (pos, invf)
    for l in range(depth):
        wq, wkv = _up_weights(w_uq[l], w_ukv[l])
        ln = (row(ln_in_g), row(ln_in_b)) if l == 0 else None
        outs = _prep(h, rope_in, ln, wa, wu, row(b_glu[l]), row(g_q[l]), wq, row(g_kv[l]), wkv,
                     w_dw[l], row(b_dw[l]), row(g_cln[l]), row(b_cln[l]), w2, w_out, l, seq)
        if l == 0:
            h, cos_t, sin_t = outs[:3]
            rope_in, outs = (cos_t, sin_t), outs[3:]
        q, k, v, conv, w2_b, wo_b = outs
        attn, w1_b = _attention(q, k, v, w1, l, batch, seq)
        h = _outproj(attn, conv, h, wo_b, row(ln1_g[l]), row(ln1_b[l]), alpha)
        h = _mlp(h, w1_b, w2_b, row(ln2_g[l]), row(ln2_b[l]), alpha)
    return h.reshape(batch, seq, d)
```

```python
import functools

import jax
import jax.numpy as jnp
from jax import lax
from jax.experimental import pallas as pl
from jax.experimental.pallas import tpu as pltpu

D_MODEL = 2048
CHUNK = 64
MLA_HEADS = 8
QK_NOPE = 128
QK_ROPE = 64
V_HEAD = 128
Q_LORA = 512
KV_LORA = 256
MLA_WIDTH = MLA_HEADS * V_HEAD
CONV_WIDTH = D_MODEL - MLA_WIDTH
CONV_K = 31
D_FF = 4 * D_MODEL
ROPE_THETA = 10000.0
LN_EPS = 1e-5
RMS_EPS = 1e-6

LANES = 128
SUBLANES = 8
HALF_ROPE = QK_ROPE // 2
W_IN_TC = 256
A_COLS = Q_LORA + KV_LORA + 2 * QK_ROPE
CONV_HALO = 32

PREP_TM = 256
PREP_CG = 256
ATTN_TQ = 256
CONV_TR = 32
OUT_TM = 512
OUT_TR = 128
MLP_TM = 1024
MLP_TF = 512
MLP_TN = 512

BF16 = jnp.bfloat16
F32 = jnp.float32


def _layer_norm(x, g, b):
    mu = jnp.mean(x, axis=-1, keepdims=True)
    xc = x - mu
    var = jnp.mean(xc * xc, axis=-1, keepdims=True)
    return xc * lax.rsqrt(var + LN_EPS) * g + b


def _rms_norm(x, g):
    return x * lax.rsqrt(jnp.mean(x * x, axis=-1, keepdims=True) + RMS_EPS) * g


def _dot(a, b):
    return jnp.dot(a, b, preferred_element_type=F32)


def _const_spec(shape):
    return pl.BlockSpec(shape, lambda *_: (0,) * len(shape))


def _conv_rows(win_ref, shift_ref, r0, tr, c0, c1, w_ref, b_ref):
    lead = CONV_HALO - (CONV_K - 1)
    blocks = []
    for ci, c in enumerate(range(c0, c1, LANES)):
        slab = win_ref[r0:r0 + tr + CONV_HALO, c:c + LANES]
        y = jnp.broadcast_to(b_ref[:, c:c + LANES], (tr, LANES))
        for b in range(SUBLANES):
            z = None
            nz = tr + (SUBLANES if b else 0)
            for a in range((lead + CONV_K - 1) // SUBLANES + 1):
                k = SUBLANES * a + b - lead
                if 0 <= k < CONV_K:
                    term = slab[SUBLANES * a:SUBLANES * a + nz, :] * w_ref[k:k + 1, c:c + LANES]
                    z = term if z is None else z + term
            if b:
                slot = (r0 // tr) * 2 + ci % 2
                shift_ref[slot, b, :, :] = z[b:b + tr, :]
                z = shift_ref[slot, b, :, :]
            y = y + z
        blocks.append(y)
    return jnp.concatenate(blocks, axis=1)


def _prep_kernel(apply_ln, tiles_per_seq, *refs):
    if apply_ln:
        (x_ref, pos_ref, invf_ref, lng_ref, lnb_ref, wa_ref, wu_ref, bglu_ref, gq_ref, wq_ref,
         gkv_ref, wkv_ref, wdw_ref, bdw_ref, gcln_ref, bcln_ref, w2_ref, wo_ref,
         h_ref, cos_ref, sin_ref, q_ref, k_ref, v_ref, conv_ref, w2b_ref, wob_ref,
         win_ref, pre_ref, shift_ref) = refs
    else:
        (x_ref, cos_ref, sin_ref, wa_ref, wu_ref, bglu_ref, gq_ref, wq_ref,
         gkv_ref, wkv_ref, wdw_ref, bdw_ref, gcln_ref, bcln_ref, w2_ref, wo_ref,
         q_ref, k_ref, v_ref, conv_ref, w2b_ref, wob_ref, win_ref, pre_ref, shift_ref) = refs
    tm = PREP_TM

    @pl.when(pl.program_id(0) % tiles_per_seq == 0)
    def _():
        win_ref[0:CONV_HALO, :] = jnp.zeros((CONV_HALO, CONV_WIDTH), F32)

    w2b_ref[...] = w2_ref[...].astype(BF16)
    wob_ref[...] = wo_ref[...].astype(BF16)

    nope = lambda h: slice(2 * h * LANES, (2 * h + 1) * LANES)
    rope = lambda h: slice((2 * h + 1) * LANES, (2 * h + 2) * LANES)
    x = x_ref[...]
    if apply_ln:
        x = _layer_norm(x, lng_ref[...], lnb_ref[...])
        h_ref[...] = x
    xb = x.astype(BF16)

    for c0 in range(0, CONV_WIDTH, PREP_CG):
        c1 = c0 + PREP_CG
        ua = _dot(xb, wu_ref[:, c0:c1]) + bglu_ref[:, c0:c1]
        ug = _dot(xb, wu_ref[:, CONV_WIDTH + c0:CONV_WIDTH + c1]) + bglu_ref[:, CONV_WIDTH + c0:CONV_WIDTH + c1]
        win_ref[CONV_HALO:CONV_HALO + tm, c0:c1] = ua * jax.nn.sigmoid(ug)
        for r0 in range(0, tm, CONV_TR):
            pre_ref[r0:r0 + CONV_TR, c0:c1] = _conv_rows(win_ref, shift_ref, r0, CONV_TR, c0, c1,
                                                         wdw_ref, bdw_ref)
    for r0 in range(0, tm, CONV_TR):
        y = _layer_norm(pre_ref[r0:r0 + CONV_TR, :], gcln_ref[...], bcln_ref[...])
        conv_ref[r0:r0 + CONV_TR, :] = (y * jax.nn.sigmoid(y)).astype(BF16)

    if apply_ln:
        ang = pos_ref[...].astype(F32) * invf_ref[...]
        cos_t = jnp.cos(ang)
        sin_t = jnp.sin(ang)
        cos_ref[...] = cos_t
        sin_ref[...] = sin_t
    else:
        cos_t = cos_ref[...]
        sin_t = sin_ref[...]

    za = _dot(xb, wa_ref[...])
    cq = za[:, :Q_LORA]
    ckv = za[:, Q_LORA:Q_LORA + KV_LORA]
    krb = za[:, Q_LORA + KV_LORA:]

    lane = lax.broadcasted_iota(jnp.int32, krb.shape, 1)
    t = krb * jnp.where(lane < QK_ROPE, cos_t, sin_t)
    kro = t + pltpu.roll(t, QK_ROPE, 1)
    kr = jnp.where(lane < QK_ROPE, kro, 0.0).astype(BF16)

    q = _dot(_rms_norm(cq, gq_ref[...]).astype(BF16), wq_ref[...])
    kv = _dot(_rms_norm(ckv, gkv_ref[...]).astype(BF16), wkv_ref[...])
    for h in range(MLA_HEADS):
        lo = MLA_WIDTH + h * LANES
        sw = 2 * MLA_WIDTH + h * LANES
        q_ref[:, nope(h)] = q[:, h * LANES:(h + 1) * LANES].astype(BF16)
        q_ref[:, rope(h)] = (q[:, lo:lo + LANES] * cos_t + q[:, sw:sw + LANES] * sin_t).astype(BF16)
        k_ref[:, nope(h)] = kv[:, h * LANES:(h + 1) * LANES].astype(BF16)
        k_ref[:, rope(h)] = kr
    v_ref[...] = kv[:, MLA_WIDTH:].astype(BF16)

    win_ref[0:CONV_HALO, :] = win_ref[tm:tm + CONV_HALO, :]


def _prep(x, rope_in, ln, wa, wu, bglu, gq, wq, gkv, wkv, wdw, bdw, gcln, bcln, w2, w_out, layer,
          seq):
    m = x.shape[0]
    tm = PREP_TM
    steps = m // tm
    f2, fo = D_FF // steps, D_MODEL // steps
    apply_ln = ln is not None
    row = lambda w: pl.BlockSpec((tm, w), lambda i: (i, 0))
    if apply_ln:
        in_specs = [row(D_MODEL), row(1), _const_spec((1, LANES))] + [_const_spec((1, D_MODEL))] * 2
        args = [x, *rope_in, *ln]
    else:
        in_specs = [row(D_MODEL), row(LANES), row(LANES)]
        args = [x, *rope_in]
    once = lambda a: pl.BlockSpec(a.shape, lambda i: (0, 0), pipeline_mode=pl.Buffered(1))
    consts = [bglu, gq, wq, gkv, wkv, wdw, bdw, gcln, bcln]
    in_specs += [once(wa), once(wu)]
    in_specs += [once(a) if a is wq or a is wkv else _const_spec(a.shape) for a in consts]
    in_specs += [pl.BlockSpec((None, f2, D_MODEL), lambda i: (layer, i, 0)),
                 pl.BlockSpec((None, fo, D_MODEL), lambda i: (layer, i, 0))]
    args += [wa, wu] + consts + [w2, w_out]
    out_shape, out_specs = [], []
    if apply_ln:
        for width in (D_MODEL, LANES, LANES):
            out_shape.append(jax.ShapeDtypeStruct((m, width), F32))
            out_specs.append(row(width))
    for width in (2 * MLA_WIDTH, 2 * MLA_WIDTH, MLA_WIDTH, CONV_WIDTH):
        out_shape.append(jax.ShapeDtypeStruct((m, width), BF16))
        out_specs.append(row(width))
    out_shape += [jax.ShapeDtypeStruct((D_FF, D_MODEL), BF16),
                  jax.ShapeDtypeStruct((D_MODEL, D_MODEL), BF16)]
    out_specs += [pl.BlockSpec((f2, D_MODEL), lambda i: (i, 0)),
                  pl.BlockSpec((fo, D_MODEL), lambda i: (i, 0))]
    return pl.pallas_call(
        functools.partial(_prep_kernel, apply_ln, seq // tm),
        grid=(steps,),
        in_specs=in_specs,
        out_specs=out_specs,
        out_shape=out_shape,
        scratch_shapes=[pltpu.VMEM((tm + CONV_HALO, CONV_WIDTH), F32),
                        pltpu.VMEM((tm, CONV_WIDTH), F32),
                        pltpu.VMEM((2 * tm // CONV_TR, SUBLANES, CONV_TR, LANES), F32)],
        compiler_params=pltpu.CompilerParams(
            dimension_semantics=("arbitrary",), vmem_limit_bytes=52 * 1024 * 1024),
        name="prep_ln" if apply_ln else "prep",
    )(*args)


def _attn_kernel(q_ref, k_ref, v_ref, w1_ref, o_ref, w1b_ref, s_ref, mx_ref, l_ref, acc_ref):
    w1b_ref[...] = w1_ref[...].astype(BF16)

    i = pl.program_id(1)
    tq = ATTN_TQ
    c = (QK_NOPE + QK_ROPE) ** -0.5 * 1.4426950408889634
    hq = lambda h: slice(2 * h * LANES, 2 * (h + 1) * LANES)
    hv = lambda h: slice(h * LANES, (h + 1) * LANES)

    mx_ref[...] = jnp.full(mx_ref.shape, -jnp.inf, F32)

    def scores(j, masked):
        r0 = pl.multiple_of(j * tq, tq)
        for h in range(MLA_HEADS):
            s = lax.dot_general(q_ref[:, hq(h)], k_ref[pl.ds(r0, tq), hq(h)],
                                (((1,), (1,)), ((), ())), preferred_element_type=F32) * c
            if masked:
                col = lax.broadcasted_iota(jnp.int32, (CHUNK, tq), 1)
                s = jnp.concatenate(
                    [jnp.where(col < (r + 1) * CHUNK, s[r * CHUNK:(r + 1) * CHUNK, :], -1e30)
                     for r in range(tq // CHUNK)], axis=0)
            s_ref[h, j] = s
            mx_ref[h] = jnp.maximum(mx_ref[h], jnp.maximum(s[:, :LANES], s[:, LANES:]))

    def weighted(j):
        r0 = pl.multiple_of(j * tq, tq)
        for h in range(MLA_HEADS):
            m = mx_ref[h]
            p = jnp.exp2(s_ref[h, j] - jnp.concatenate([m, m], axis=1))
            l_ref[h] += p[:, :LANES] + p[:, LANES:]
            acc_ref[h] += _dot(p.astype(BF16), v_ref[pl.ds(r0, tq), hv(h)])

    def pairs(fn, n):
        def four(t, _):
            for u in range(4):
                fn(4 * t + u)
            return 0
        lax.fori_loop(0, n // 4, four, 0)

        @pl.when(n % 4 >= 2)
        def _():
            fn((n // 4) * 4)
            fn((n // 4) * 4 + 1)

        @pl.when(n % 2 == 1)
        def _():
            fn(n - 1)

    pairs(lambda j: scores(j, False), i)
    scores(i, True)

    for h in range(MLA_HEADS):
        mx_ref[h] = jnp.broadcast_to(jnp.max(mx_ref[h], axis=-1, keepdims=True), (tq, LANES))
    l_ref[...] = jnp.zeros(l_ref.shape, F32)
    acc_ref[...] = jnp.zeros(acc_ref.shape, F32)

    pairs(weighted, i + 1)

    for h in range(MLA_HEADS):
        l = jnp.sum(l_ref[h], axis=-1, keepdims=True)
        o_ref[:, hv(h)] = (acc_ref[h] / l).astype(o_ref.dtype)


def _attention(q, k, v, w1, layer, batch, seq):
    m = q.shape[0]
    tq = ATTN_TQ
    nq = seq // tq
    f1 = D_FF // (batch * nq)
    step = lambda b, i: b * nq + i
    head_acc = pltpu.VMEM((MLA_HEADS, tq, LANES), F32)
    return pl.pallas_call(
        _attn_kernel,
        grid=(batch, nq),
        in_specs=[pl.BlockSpec((tq, 2 * MLA_WIDTH), lambda b, i: (step(b, i), 0)),
                  pl.BlockSpec((seq, 2 * MLA_WIDTH), lambda b, i: (b, 0)),
                  pl.BlockSpec((seq, MLA_WIDTH), lambda b, i: (b, 0)),
                  pl.BlockSpec((None, D_MODEL, f1), lambda b, i: (layer, 0, step(b, i)))],
        out_specs=[pl.BlockSpec((tq, MLA_WIDTH), lambda b, i: (step(b, i), 0)),
                   pl.BlockSpec((D_MODEL, f1), lambda b, i: (0, step(b, i)))],
        out_shape=[jax.ShapeDtypeStruct((m, MLA_WIDTH), BF16),
                   jax.ShapeDtypeStruct((D_MODEL, D_FF), BF16)],
        scratch_shapes=[pltpu.VMEM((MLA_HEADS, nq, tq, tq), F32), head_acc, head_acc, head_acc],
        compiler_params=pltpu.CompilerParams(
            dimension_semantics=("arbitrary", "arbitrary"), vmem_limit_bytes=58 * 1024 * 1024),
        name="attention",
    )(q, k, v, w1)


def _outproj_kernel(alpha, split_next, a_ref, c_ref, x_ref, w_ref, g_ref, b_ref, *rest):
    if split_next:
        wt_ref, o_ref, wa_ref, wu_ref = rest
        _w_in_split(wt_ref, wa_ref, wu_ref)
    else:
        (o_ref,) = rest
    for r in range(0, OUT_TM, OUT_TR):
        rows = slice(r, r + OUT_TR)
        mix = (_dot(a_ref[rows, :], w_ref[0:MLA_WIDTH, :])
               + _dot(c_ref[rows, :], w_ref[MLA_WIDTH:D_MODEL, :]))
        o_ref[rows, :] = _layer_norm(alpha * x_ref[rows, :] + mix, g_ref[...], b_ref[...])


def _outproj(attn, conv, x, w, g, b, alpha, w_in_t=None, next_layer=None):
    m = x.shape[0]
    tm = OUT_TM
    steps = m // tm
    row = lambda wd: pl.BlockSpec((tm, wd), lambda i: (i, 0))
    in_specs = [row(MLA_WIDTH), row(CONV_WIDTH), row(D_MODEL),
                pl.BlockSpec(w.shape, lambda i: (0, 0), pipeline_mode=pl.Buffered(1)),
                _const_spec(g.shape), _const_spec(b.shape)]
    out_specs = [row(D_MODEL)]
    out_shape = [jax.ShapeDtypeStruct((m, D_MODEL), F32)]
    args = [attn, conv, x, w, g, b]
    split_next = w_in_t is not None
    if split_next:
        tc = D_MODEL // steps
        in_specs.append(pl.BlockSpec((None, w_in_t.shape[1], tc), lambda i: (next_layer, 0, i)))
        out_specs += [pl.BlockSpec((tc, A_COLS), lambda i: (i, 0)),
                      pl.BlockSpec((tc, 2 * CONV_WIDTH), lambda i: (i, 0))]
        out_shape += [jax.ShapeDtypeStruct((D_MODEL, A_COLS), BF16),
                      jax.ShapeDtypeStruct((D_MODEL, 2 * CONV_WIDTH), BF16)]
        args.append(w_in_t)
    return pl.pallas_call(
        functools.partial(_outproj_kernel, alpha, split_next),
        grid=(steps,),
        in_specs=in_specs,
        out_specs=out_specs,
        out_shape=out_shape,
        compiler_params=pltpu.CompilerParams(
            dimension_semantics=("arbitrary",), vmem_limit_bytes=56 * 1024 * 1024),
        name="outproj",
    )(*args)


def _mlp_kernel(alpha, x_ref, w1_ref, w2_ref, g_ref, b_ref, o_ref, xb_ref):
    f = pl.program_id(1)

    @pl.when(f == 0)
    def _():
        x = x_ref[...]
        xb_ref[...] = x.astype(BF16)
        o_ref[...] = alpha * x

    h = jnp.maximum(_dot(xb_ref[...], w1_ref[...]), 0.0)
    hb = (h * h).astype(BF16)
    for n in range(0, D_MODEL, MLP_TN):
        o_ref[:, n:n + MLP_TN] += _dot(hb, w2_ref[:, n:n + MLP_TN])

    @pl.when(f == pl.num_programs(1) - 1)
    def _():
        o_ref[...] = _layer_norm(o_ref[...], g_ref[...], b_ref[...])


def _mlp(x, w1, w2, g, b, alpha):
    m = x.shape[0]
    tm, tf = MLP_TM, MLP_TF
    xspec = pl.BlockSpec((tm, D_MODEL), lambda i, f: (i, 0))
    return pl.pallas_call(
        functools.partial(_mlp_kernel, alpha),
        grid=(m // tm, D_FF // tf),
        in_specs=[xspec, pl.BlockSpec((D_MODEL, tf), lambda i, f: (0, f)),
                  pl.BlockSpec((tf, D_MODEL), lambda i, f: (f, 0)),
                  _const_spec(g.shape), _const_spec(b.shape)],
        out_specs=xspec,
        out_shape=jax.ShapeDtypeStruct((m, D_MODEL), F32),
        scratch_shapes=[pltpu.VMEM((tm, D_MODEL), BF16)],
        compiler_params=pltpu.CompilerParams(
            dimension_semantics=("arbitrary", "arbitrary"), vmem_limit_bytes=58 * 1024 * 1024),
        name="mlp",
    )(x, w1, w2, g, b)


def _swap_halves(w):
    return jnp.concatenate([-w[..., HALF_ROPE:], w[..., :HALF_ROPE]], axis=-1)


def _w_in_split(wt_ref, wa_ref, wu_ref):
    kv_end = Q_LORA + KV_LORA
    kr_end = kv_end + QK_ROPE
    wa_ref[:, :kv_end] = wt_ref[:kv_end, :].T.astype(BF16)
    k1 = wt_ref[kv_end:kv_end + HALF_ROPE, :]
    k2 = wt_ref[kv_end + HALF_ROPE:kr_end, :]
    wa_ref[:, kv_end:] = jnp.concatenate([k1, k2, -k2, k1], axis=0).T.astype(BF16)
    wu_ref[...] = wt_ref[kr_end:, :].T.astype(BF16)


def _w_in_layout(w_in_t, layer):
    _, in_cols, d = w_in_t.shape
    tc = W_IN_TC
    return pl.pallas_call(
        _w_in_split,
        grid=(d // tc,),
        in_specs=[pl.BlockSpec((None, in_cols, tc), lambda i: (layer, 0, i))],
        out_specs=[pl.BlockSpec((tc, A_COLS), lambda i: (i, 0)),
                   pl.BlockSpec((tc, 2 * CONV_WIDTH), lambda i: (i, 0))],
        out_shape=[jax.ShapeDtypeStruct((d, A_COLS), BF16),
                   jax.ShapeDtypeStruct((d, 2 * CONV_WIDTH), BF16)],
        compiler_params=pltpu.CompilerParams(dimension_semantics=("arbitrary",)),
        name="w_in_layout",
    )(w_in_t)


def _up_weights(w_uq, w_ukv):
    wq3 = w_uq.reshape(Q_LORA, MLA_HEADS, QK_NOPE + QK_ROPE)
    rope = wq3[:, :, QK_NOPE:]
    pad = jnp.zeros((Q_LORA, MLA_HEADS, LANES - QK_ROPE), w_uq.dtype)
    wq = jnp.concatenate([
        wq3[:, :, :QK_NOPE].reshape(Q_LORA, MLA_WIDTH),
        jnp.concatenate([rope, pad], axis=-1).reshape(Q_LORA, MLA_HEADS * LANES),
        jnp.concatenate([_swap_halves(rope), pad], axis=-1).reshape(Q_LORA, MLA_HEADS * LANES),
    ], axis=1).astype(BF16)

    wkv3 = w_ukv.reshape(KV_LORA, MLA_HEADS, QK_NOPE + V_HEAD)
    wkv = jnp.concatenate([wkv3[:, :, :QK_NOPE].reshape(KV_LORA, MLA_WIDTH),
                           wkv3[:, :, QK_NOPE:].reshape(KV_LORA, MLA_WIDTH)], axis=1).astype(BF16)
    return wq, wkv


def kernel(x, positions, ln_in_g, ln_in_b, w_in, g_q, w_uq, g_kv, w_ukv, b_glu, w_dw, b_dw, g_cln,
           b_cln, w_out, ln1_g, ln1_b, w1, w2, ln2_g, ln2_b):
    batch, seq, d = x.shape
    depth = w_in.shape[0]
    m = batch * seq
    alpha = (2.0 * depth) ** 0.25

    inv_freq = ROPE_THETA ** (-jnp.arange(0, QK_ROPE, 2, dtype=F32) / QK_ROPE)
    invf = jnp.tile(inv_freq, LANES // HALF_ROPE).reshape(1, LANES)
    pos = positions.reshape(m, 1)
    row = lambda a: a.reshape(1, -1)

    w_in_t = jnp.swapaxes(w_in, 1, 2)
    wa, wu = _w_in_layout(w_in_t, 0)
    h = x.reshape(m, d)
    rope_in = (pos, invf)
    for l in range(depth):
        wq, wkv = _up_weights(w_uq[l], w_ukv[l])
        ln = (row(ln_in_g), row(ln_in_b)) if l == 0 else None
        outs = _prep(h, rope_in, ln, wa, wu, row(b_glu[l]), row(g_q[l]), wq, row(g_kv[l]), wkv,
                     w_dw[l], row(b_dw[l]), row(g_cln[l]), row(b_cln[l]), w2, w_out, l, seq)
        if l == 0:
            h, cos_t, sin_t = outs[:3]
            rope_in, outs = (cos_t, sin_t), outs[3:]
        q, k, v, conv, w2_b, wo_b = outs
        attn, w1_b = _attention(q, k, v, w1, l, batch, seq)
        if l + 1 < depth:
            h, wa, wu = _outproj(attn, conv, h, wo_b, row(ln1_g[l]), row(ln1_b[l]), alpha,
                                 w_in_t, l + 1)
        else:
            (h,) = _outproj(attn, conv, h, wo_b, row(ln1_g[l]), row(ln1_b[l]), alpha)
        h = _mlp(h, w1_b, w2_b, row(ln2_g[l]), row(ln2_b[l]), alpha)
    return h.reshape(batch, seq, d)
```

```python
import functools

import jax
import jax.numpy as jnp
from jax import lax
from jax.experimental import pallas as pl
from jax.experimental.pallas import tpu as pltpu

D_MODEL = 2048
CHUNK = 64
MLA_HEADS = 8
QK_NOPE = 128
QK_ROPE = 64
V_HEAD = 128
Q_LORA = 512
KV_LORA = 256
MLA_WIDTH = MLA_HEADS * V_HEAD
CONV_WIDTH = D_MODEL - MLA_WIDTH
CONV_K = 31
D_FF = 4 * D_MODEL
ROPE_THETA = 10000.0
LN_EPS = 1e-5
RMS_EPS = 1e-6

LANES = 128
SUBLANES = 8
HALF_ROPE = QK_ROPE // 2
W_IN_TC = 256
A_COLS = Q_LORA + KV_LORA + 2 * QK_ROPE
CONV_HALO = 32

PREP_TM = 256
PREP_CG = 256
ATTN_TQ = 256
CONV_TR = 32
OUT_TM = 512
OUT_TR = 128
MLP_TM = 1024
MLP_TF = 512
MLP_TN = 512

BF16 = jnp.bfloat16
F32 = jnp.float32


def _layer_norm(x, g, b):
    mu = jnp.mean(x, axis=-1, keepdims=True)
    xc = x - mu
    var = jnp.mean(xc * xc, axis=-1, keepdims=True)
    return xc * lax.rsqrt(var + LN_EPS) * g + b


def _rms_norm(x, g):
    return x * lax.rsqrt(jnp.mean(x * x, axis=-1, keepdims=True) + RMS_EPS) * g


def _dot(a, b):
    return jnp.dot(a, b, preferred_element_type=F32)


def _const_spec(shape):
    return pl.BlockSpec(shape, lambda *_: (0,) * len(shape))


def _conv_rows(win_ref, shift_ref, r0, tr, c0, c1, w_ref, b_ref):
    lead = CONV_HALO - (CONV_K - 1)
    blocks = []
    for ci, c in enumerate(range(c0, c1, LANES)):
        slab = win_ref[r0:r0 + tr + CONV_HALO, c:c + LANES]
        y = jnp.broadcast_to(b_ref[:, c:c + LANES], (tr, LANES))
        for b in range(SUBLANES):
            z = None
            nz = tr + (SUBLANES if b else 0)
            for a in range((lead + CONV_K - 1) // SUBLANES + 1):
                k = SUBLANES * a + b - lead
                if 0 <= k < CONV_K:
                    term = slab[SUBLANES * a:SUBLANES * a + nz, :] * w_ref[k:k + 1, c:c + LANES]
                    z = term if z is None else z + term
            if b:
                slot = (r0 // tr) * 2 + ci % 2
                shift_ref[slot, b, :, :] = z[b:b + tr, :]
                z = shift_ref[slot, b, :, :]
            y = y + z
        blocks.append(y)
    return jnp.concatenate(blocks, axis=1)


def _prep_kernel(apply_ln, tiles_per_seq, *refs):
    if apply_ln:
        (x_ref, pos_ref, invf_ref, lng_ref, lnb_ref, wa_ref, wu_ref, bglu_ref, gq_ref, wq_ref,
         gkv_ref, wkv_ref, wdw_ref, bdw_ref, gcln_ref, bcln_ref, w2_ref, wo_ref,
         h_ref, cos_ref, sin_ref, q_ref, k_ref, v_ref, conv_ref, w2b_ref, wob_ref,
         win_ref, pre_ref, shift_ref) = refs
    else:
        (x_ref, cos_ref, sin_ref, wa_ref, wu_ref, bglu_ref, gq_ref, wq_ref,
         gkv_ref, wkv_ref, wdw_ref, bdw_ref, gcln_ref, bcln_ref, w2_ref, wo_ref,
         q_ref, k_ref, v_ref, conv_ref, w2b_ref, wob_ref, win_ref, pre_ref, shift_ref) = refs
    tm = PREP_TM

    @pl.when(pl.program_id(0) % tiles_per_seq == 0)
    def _():
        win_ref[0:CONV_HALO, :] = jnp.zeros((CONV_HALO, CONV_WIDTH), F32)

    w2b_ref[...] = w2_ref[...].astype(BF16)
    wob_ref[...] = wo_ref[...].astype(BF16)

    nope = lambda h: slice(2 * h * LANES, (2 * h + 1) * LANES)
    rope = lambda h: slice((2 * h + 1) * LANES, (2 * h + 2) * LANES)
    x = x_ref[...]
    if apply_ln:
        x = _layer_norm(x, lng_ref[...], lnb_ref[...])
        h_ref[...] = x
    xb = x.astype(BF16)

    for c0 in range(0, CONV_WIDTH, PREP_CG):
        c1 = c0 + PREP_CG
        ua = _dot(xb, wu_ref[:, c0:c1]) + bglu_ref[:, c0:c1]
        ug = _dot(xb, wu_ref[:, CONV_WIDTH + c0:CONV_WIDTH + c1]) + bglu_ref[:, CONV_WIDTH + c0:CONV_WIDTH + c1]
        win_ref[CONV_HALO:CONV_HALO + tm, c0:c1] = ua * jax.nn.sigmoid(ug)
        for r0 in range(0, tm, CONV_TR):
            pre_ref[r0:r0 + CONV_TR, c0:c1] = _conv_rows(win_ref, shift_ref, r0, CONV_TR, c0, c1,
                                                         wdw_ref, bdw_ref)
    for r0 in range(0, tm, CONV_TR):
        y = _layer_norm(pre_ref[r0:r0 + CONV_TR, :], gcln_ref[...], bcln_ref[...])
        conv_ref[r0:r0 + CONV_TR, :] = (y * jax.nn.sigmoid(y)).astype(BF16)

    if apply_ln:
        ang = pos_ref[...].astype(F32) * invf_ref[...]
        cos_t = jnp.cos(ang)
        sin_t = jnp.sin(ang)
        cos_ref[...] = cos_t
        sin_ref[...] = sin_t
    else:
        cos_t = cos_ref[...]
        sin_t = sin_ref[...]

    za = _dot(xb, wa_ref[...])
    cq = za[:, :Q_LORA]
    ckv = za[:, Q_LORA:Q_LORA + KV_LORA]
    krb = za[:, Q_LORA + KV_LORA:]

    lane = lax.broadcasted_iota(jnp.int32, krb.shape, 1)
    t = krb * jnp.where(lane < QK_ROPE, cos_t, sin_t)
    kro = t + pltpu.roll(t, QK_ROPE, 1)
    kr = jnp.where(lane < QK_ROPE, kro, 0.0).astype(BF16)

    q = _dot(_rms_norm(cq, gq_ref[...]).astype(BF16), wq_ref[...])
    kv = _dot(_rms_norm(ckv, gkv_ref[...]).astype(BF16), wkv_ref[...])
    for h in range(MLA_HEADS):
        lo = MLA_WIDTH + h * LANES
        sw = 2 * MLA_WIDTH + h * LANES
        q_ref[:, nope(h)] = q[:, h * LANES:(h + 1) * LANES].astype(BF16)
        q_ref[:, rope(h)] = (q[:, lo:lo + LANES] * cos_t + q[:, sw:sw + LANES] * sin_t).astype(BF16)
        k_ref[:, nope(h)] = kv[:, h * LANES:(h + 1) * LANES].astype(BF16)
        k_ref[:, rope(h)] = kr
    v_ref[...] = kv[:, MLA_WIDTH:].astype(BF16)

    win_ref[0:CONV_HALO, :] = win_ref[tm:tm + CONV_HALO, :]


def _prep(x, rope_in, ln, wa, wu, bglu, gq, wq, gkv, wkv, wdw, bdw, gcln, bcln, w2, w_out, layer,
          seq):
    m = x.shape[0]
    tm = PREP_TM
    steps = m // tm
    f2, fo = D_FF // steps, D_MODEL // steps
    apply_ln = ln is not None
    row = lambda w: pl.BlockSpec((tm, w), lambda i: (i, 0))
    if apply_ln:
        in_specs = [row(D_MODEL), row(1), _const_spec((1, LANES))] + [_const_spec((1, D_MODEL))] * 2
        args = [x, *rope_in, *ln]
    else:
        in_specs = [row(D_MODEL), row(LANES), row(LANES)]
        args = [x, *rope_in]
    slab = lambda a: pl.BlockSpec((None,) + a.shape[1:], lambda i: (layer, 0, 0),
                                  pipeline_mode=pl.Buffered(1))
    once = lambda a: pl.BlockSpec(a.shape, lambda i: (0, 0), pipeline_mode=pl.Buffered(1))
    consts = [bglu, gq, wq, gkv, wkv, wdw, bdw, gcln, bcln]
    in_specs += [slab(wa), slab(wu)]
    in_specs += [once(a) if a is wq or a is wkv else _const_spec(a.shape) for a in consts]
    in_specs += [pl.BlockSpec((None, f2, D_MODEL), lambda i: (layer, i, 0)),
                 pl.BlockSpec((None, fo, D_MODEL), lambda i: (layer, i, 0))]
    args += [wa, wu] + consts + [w2, w_out]
    out_shape, out_specs = [], []
    if apply_ln:
        for width in (D_MODEL, LANES, LANES):
            out_shape.append(jax.ShapeDtypeStruct((m, width), F32))
            out_specs.append(row(width))
    for width in (2 * MLA_WIDTH, 2 * MLA_WIDTH, MLA_WIDTH, CONV_WIDTH):
        out_shape.append(jax.ShapeDtypeStruct((m, width), BF16))
        out_specs.append(row(width))
    out_shape += [jax.ShapeDtypeStruct((D_FF, D_MODEL), BF16),
                  jax.ShapeDtypeStruct((D_MODEL, D_MODEL), BF16)]
    out_specs += [pl.BlockSpec((f2, D_MODEL), lambda i: (i, 0)),
                  pl.BlockSpec((fo, D_MODEL), lambda i: (i, 0))]
    return pl.pallas_call(
        functools.partial(_prep_kernel, apply_ln, seq // tm),
        grid=(steps,),
        in_specs=in_specs,
        out_specs=out_specs,
        out_shape=out_shape,
        scratch_shapes=[pltpu.VMEM((tm + CONV_HALO, CONV_WIDTH), F32),
                        pltpu.VMEM((tm, CONV_WIDTH), F32),
                        pltpu.VMEM((2 * tm // CONV_TR, SUBLANES, CONV_TR, LANES), F32)],
        compiler_params=pltpu.CompilerParams(
            dimension_semantics=("arbitrary",), vmem_limit_bytes=52 * 1024 * 1024),
        name="prep_ln" if apply_ln else "prep",
    )(*args)


def _attn_kernel(q_ref, k_ref, v_ref, w1_ref, o_ref, w1b_ref, s_ref, mx_ref, acc_ref):
    w1b_ref[...] = w1_ref[...].astype(BF16)

    i = pl.program_id(1)
    tq = ATTN_TQ
    c = (QK_NOPE + QK_ROPE) ** -0.5 * 1.4426950408889634
    hq = lambda h: slice(2 * h * LANES, 2 * (h + 1) * LANES)
    hv = lambda h: slice(h * LANES, (h + 1) * LANES)

    mx_ref[...] = jnp.full(mx_ref.shape, -jnp.inf, F32)

    def scores(j, masked):
        r0 = pl.multiple_of(j * tq, tq)
        for h in range(MLA_HEADS):
            s = lax.dot_general(q_ref[:, hq(h)], k_ref[pl.ds(r0, tq), hq(h)],
                                (((1,), (1,)), ((), ())), preferred_element_type=F32) * c
            if masked:
                col = lax.broadcasted_iota(jnp.int32, (CHUNK, tq), 1)
                s = jnp.concatenate(
                    [jnp.where(col < (r + 1) * CHUNK, s[r * CHUNK:(r + 1) * CHUNK, :], -1e30)
                     for r in range(tq // CHUNK)], axis=0)
            s_ref[h, j] = s
            mx_ref[h] = jnp.maximum(mx_ref[h], jnp.maximum(s[:, :LANES], s[:, LANES:]))

    ones = jnp.ones((tq, LANES), BF16)

    def weighted(j):
        r0 = pl.multiple_of(j * tq, tq)
        for h in range(MLA_HEADS):
            m = mx_ref[h]
            p = jnp.exp2(s_ref[h, j] - jnp.concatenate([m, m], axis=1))
            v1 = jnp.concatenate([v_ref[pl.ds(r0, tq), hv(h)], ones], axis=1)
            acc_ref[h] += _dot(p.astype(BF16), v1)

    def pairs(fn, n):
        def four(t, _):
            for u in range(4):
                fn(4 * t + u)
            return 0
        lax.fori_loop(0, n // 4, four, 0)

        @pl.when(n % 4 >= 2)
        def _():
            fn((n // 4) * 4)
            fn((n // 4) * 4 + 1)

        @pl.when(n % 2 == 1)
        def _():
            fn(n - 1)

    pairs(lambda j: scores(j, False), i)
    scores(i, True)

    for h in range(MLA_HEADS):
        mx_ref[h] = jnp.broadcast_to(jnp.max(mx_ref[h], axis=-1, keepdims=True), (tq, LANES))
    acc_ref[...] = jnp.zeros(acc_ref.shape, F32)

    pairs(weighted, i + 1)

    for h in range(MLA_HEADS):
        o_ref[:, hv(h)] = (acc_ref[h, :, :LANES] / acc_ref[h, :, LANES:]).astype(o_ref.dtype)


def _attention(q, k, v, w1, layer, batch, seq):
    m = q.shape[0]
    tq = ATTN_TQ
    nq = seq // tq
    f1 = D_FF // (batch * nq)
    step = lambda b, i: b * nq + i
    head_acc = pltpu.VMEM((MLA_HEADS, tq, LANES), F32)
    return pl.pallas_call(
        _attn_kernel,
        grid=(batch, nq),
        in_specs=[pl.BlockSpec((tq, 2 * MLA_WIDTH), lambda b, i: (step(b, i), 0)),
                  pl.BlockSpec((seq, 2 * MLA_WIDTH), lambda b, i: (b, 0)),
                  pl.BlockSpec((seq, MLA_WIDTH), lambda b, i: (b, 0)),
                  pl.BlockSpec((None, D_MODEL, f1), lambda b, i: (layer, 0, step(b, i)))],
        out_specs=[pl.BlockSpec((tq, MLA_WIDTH), lambda b, i: (step(b, i), 0)),
                   pl.BlockSpec((D_MODEL, f1), lambda b, i: (0, step(b, i)))],
        out_shape=[jax.ShapeDtypeStruct((m, MLA_WIDTH), BF16),
                   jax.ShapeDtypeStruct((D_MODEL, D_FF), BF16)],
        scratch_shapes=[pltpu.VMEM((MLA_HEADS, nq, tq, tq), F32), head_acc,
                        pltpu.VMEM((MLA_HEADS, tq, 2 * LANES), F32)],
        compiler_params=pltpu.CompilerParams(
            dimension_semantics=("arbitrary", "arbitrary"), vmem_limit_bytes=58 * 1024 * 1024),
        name="attention",
    )(q, k, v, w1)


def _outproj_kernel(alpha, a_ref, c_ref, x_ref, w_ref, g_ref, b_ref, o_ref):
    for r in range(0, OUT_TM, OUT_TR):
        rows = slice(r, r + OUT_TR)
        mix = (_dot(a_ref[rows, :], w_ref[0:MLA_WIDTH, :])
               + _dot(c_ref[rows, :], w_ref[MLA_WIDTH:D_MODEL, :]))
        o_ref[rows, :] = _layer_norm(alpha * x_ref[rows, :] + mix, g_ref[...], b_ref[...])


def _outproj(attn, conv, x, w, g, b, alpha):
    m = x.shape[0]
    tm = OUT_TM
    row = lambda wd: pl.BlockSpec((tm, wd), lambda i: (i, 0))
    return pl.pallas_call(
        functools.partial(_outproj_kernel, alpha),
        grid=(m // tm,),
        in_specs=[row(MLA_WIDTH), row(CONV_WIDTH), row(D_MODEL),
                  pl.BlockSpec(w.shape, lambda i: (0, 0), pipeline_mode=pl.Buffered(1)),
                  _const_spec(g.shape), _const_spec(b.shape)],
        out_specs=row(D_MODEL),
        out_shape=jax.ShapeDtypeStruct((m, D_MODEL), F32),
        compiler_params=pltpu.CompilerParams(
            dimension_semantics=("arbitrary",), vmem_limit_bytes=56 * 1024 * 1024),
        name="outproj",
    )(attn, conv, x, w, g, b)


def _mlp_kernel(alpha, x_ref, w1_ref, w2_ref, g_ref, b_ref, o_ref, xb_ref):
    f = pl.program_id(1)

    @pl.when(f == 0)
    def _():
        x = x_ref[...]
        xb_ref[...] = x.astype(BF16)
        o_ref[...] = alpha * x

    h = jnp.maximum(_dot(xb_ref[...], w1_ref[...]), 0.0)
    hb = (h * h).astype(BF16)
    for n in range(0, D_MODEL, MLP_TN):
        o_ref[:, n:n + MLP_TN] += _dot(hb, w2_ref[:, n:n + MLP_TN])

    @pl.when(f == pl.num_programs(1) - 1)
    def _():
        o_ref[...] = _layer_norm(o_ref[...], g_ref[...], b_ref[...])


def _mlp(x, w1, w2, g, b, alpha):
    m = x.shape[0]
    tm, tf = MLP_TM, MLP_TF
    xspec = pl.BlockSpec((tm, D_MODEL), lambda i, f: (i, 0))
    return pl.pallas_call(
        functools.partial(_mlp_kernel, alpha),
        grid=(m // tm, D_FF // tf),
        in_specs=[xspec, pl.BlockSpec((D_MODEL, tf), lambda i, f: (0, f)),
                  pl.BlockSpec((tf, D_MODEL), lambda i, f: (f, 0)),
                  _const_spec(g.shape), _const_spec(b.shape)],
        out_specs=xspec,
        out_shape=jax.ShapeDtypeStruct((m, D_MODEL), F32),
        scratch_shapes=[pltpu.VMEM((tm, D_MODEL), BF16)],
        compiler_params=pltpu.CompilerParams(
            dimension_semantics=("arbitrary", "arbitrary"), vmem_limit_bytes=58 * 1024 * 1024),
        name="mlp",
    )(x, w1, w2, g, b)


def _swap_halves(w):
    return jnp.concatenate([-w[..., HALF_ROPE:], w[..., :HALF_ROPE]], axis=-1)


def _w_in_kernel(wt_ref, wa_ref, wu_ref):
    kv_end = Q_LORA + KV_LORA
    kr_end = kv_end + QK_ROPE
    wa_ref[:, :kv_end] = wt_ref[:kv_end, :].T.astype(BF16)
    k1 = wt_ref[kv_end:kv_end + HALF_ROPE, :]
    k2 = wt_ref[kv_end + HALF_ROPE:kr_end, :]
    wa_ref[:, kv_end:] = jnp.concatenate([k1, k2, -k2, k1], axis=0).T.astype(BF16)
    wu_ref[...] = wt_ref[kr_end:, :].T.astype(BF16)


def _w_in_layout(w_in):
    depth, d, in_cols = w_in.shape
    w_in_t = jnp.swapaxes(w_in, 1, 2)
    tc = W_IN_TC
    return pl.pallas_call(
        _w_in_kernel,
        grid=(depth, d // tc),
        in_specs=[pl.BlockSpec((None, in_cols, tc), lambda l, i: (l, 0, i))],
        out_specs=[pl.BlockSpec((None, tc, A_COLS), lambda l, i: (l, i, 0)),
                   pl.BlockSpec((None, tc, 2 * CONV_WIDTH), lambda l, i: (l, i, 0))],
        out_shape=[jax.ShapeDtypeStruct((depth, d, A_COLS), BF16),
                   jax.ShapeDtypeStruct((depth, d, 2 * CONV_WIDTH), BF16)],
        compiler_params=pltpu.CompilerParams(dimension_semantics=("arbitrary", "arbitrary")),
        name="w_in_layout",
    )(w_in_t)


def _up_weights(w_uq, w_ukv):
    wq3 = w_uq.reshape(Q_LORA, MLA_HEADS, QK_NOPE + QK_ROPE)
    rope = wq3[:, :, QK_NOPE:]
    pad = jnp.zeros((Q_LORA, MLA_HEADS, LANES - QK_ROPE), w_uq.dtype)
    wq = jnp.concatenate([
        wq3[:, :, :QK_NOPE].reshape(Q_LORA, MLA_WIDTH),
        jnp.concatenate([rope, pad], axis=-1).reshape(Q_LORA, MLA_HEADS * LANES),
        jnp.concatenate([_swap_halves(rope), pad], axis=-1).reshape(Q_LORA, MLA_HEADS * LANES),
    ], axis=1).astype(BF16)

    wkv3 = w_ukv.reshape(KV_LORA, MLA_HEADS, QK_NOPE + V_HEAD)
    wkv = jnp.concatenate([wkv3[:, :, :QK_NOPE].reshape(KV_LORA, MLA_WIDTH),
                           wkv3[:, :, QK_NOPE:].reshape(KV_LORA, MLA_WIDTH)], axis=1).astype(BF16)
    return wq, wkv


def kernel(x, positions, ln_in_g, ln_in_b, w_in, g_q, w_uq, g_kv, w_ukv, b_glu, w_dw, b_dw, g_cln,
           b_cln, w_out, ln1_g, ln1_b, w1, w2, ln2_g, ln2_b):
    batch, seq, d = x.shape
    depth = w_in.shape[0]
    m = batch * seq
    alpha = (2.0 * depth) ** 0.25

    inv_freq = ROPE_THETA ** (-jnp.arange(0, QK_ROPE, 2, dtype=F32) / QK_ROPE)
    invf = jnp.tile(inv_freq, LANES // HALF_ROPE).reshape(1, LANES)
    pos = positions.reshape(m, 1)
    row = lambda a: a.reshape(1, -1)

    wa, wu = _w_in_layout(w_in)
    h = x.reshape(m, d)
    rope_in = (pos, invf)
    for l in range(depth):
        wq, wkv = _up_weights(w_uq[l], w_ukv[l])
        ln = (row(ln_in_g), row(ln_in_b)) if l == 0 else None
        outs = _prep(h, rope_in, ln, wa, wu, row(b_glu[l]), row(g_q[l]), wq, row(g_kv[l]), wkv,
                     w_dw[l], row(b_dw[l]), row(g_cln[l]), row(b_cln[l]), w2, w_out, l, seq)
        if l == 0:
            h, cos_t, sin_t = outs[:3]
            rope_in, outs = (cos_t, sin_t), outs[3:]
        q, k, v, conv, w2_b, wo_b = outs
        attn, w1_b = _attention(q, k, v, w1, l, batch, seq)
        h = _outproj(attn, conv, h, wo_b, row(ln1_g[l]), row(ln1_b[l]), alpha)
        h = _mlp(h, w1_b, w2_b, row(ln2_g[l]), row(ln2_b[l]), alpha)
    return h.reshape(batch, seq, d)
```
